```python
import math
import jax, jax.numpy as jnp
from jax import lax
import numpy as np

D_MODEL = 1024
BATCH = 4
SEQ = 8192
DEPTH = 2

HEAD_DIM = 64
A_HEADS = 6
A_PATTERNS = ((128, 1), (512, 4), (2048, 16))
A_BLOCK = 128
B_HEADS = 4
B_KEY_DIM = 128
B_VAL_DIM = 64
B_CHUNK = 64
C_HEADS = 6
C_BLOCK = 256
C_TOPK = 3
C_QCHUNK = 32
N_BUCKETS = 32
MAX_DISTANCE = 2048
D_FF = 2816
CONV_WIDTH = 3
PLE_DIM = 256
EPS = 1e-6
NEG_BIG = -1e30

A_WIDTH = A_HEADS * HEAD_DIM
B_QK_WIDTH = B_HEADS * B_KEY_DIM
B_V_WIDTH = B_HEADS * B_VAL_DIM
C_WIDTH = C_HEADS * HEAD_DIM
MIX_WIDTH = A_WIDTH + B_V_WIDTH + C_WIDTH
IN_SPLIT_SIZES = (A_WIDTH, A_WIDTH, A_WIDTH, B_QK_WIDTH, B_QK_WIDTH, B_V_WIDTH, B_V_WIDTH, C_WIDTH, C_WIDTH, C_WIDTH)
IN_WIDTH = sum(IN_SPLIT_SIZES)

kernel_name = 'hybrid_dilated_hgrn2_moba_convffn'


def rms_norm(x, g):
    xf = x.astype(jnp.float32)
    y = xf * lax.rsqrt(jnp.mean(xf * xf, axis=-1, keepdims=True) + EPS)
    return (y * g.astype(jnp.float32)).astype(x.dtype)


def rel_bucket(dist):
    max_exact = N_BUCKETS // 2
    d = jnp.maximum(dist, 0)
    df = jnp.maximum(d, max_exact).astype(jnp.float32)
    large = max_exact + (jnp.log(df / max_exact) / math.log(MAX_DISTANCE / max_exact)
                         * (N_BUCKETS - max_exact)).astype(jnp.int32)
    large = jnp.minimum(large, N_BUCKETS - 1)
    return jnp.where(d < max_exact, d, large)


def dilated_branch(q, k, v, bias_table, window, dilation):
    Bsz, T, H, E = q.shape
    span = dilation * A_BLOCK
    Tp = -(-T // span) * span
    nb = Tp // span

    def to_blocks(a):
        a = jnp.pad(a, ((0, 0), (0, Tp - T), (0, 0), (0, 0)))
        a = a.reshape(Bsz, Tp // dilation, dilation, H, E).transpose(0, 2, 1, 3, 4)
        return a.reshape(Bsz, dilation, nb, A_BLOCK, H, E)

    def with_prev(a):
        prev = jnp.pad(a[:, :, :-1], ((0, 0), (0, 0), (1, 0), (0, 0), (0, 0), (0, 0)))
        return jnp.concatenate([prev, a], axis=3)

    qs = to_blocks(q)
    kk = with_prev(to_blocks(k))
    vv = with_prev(to_blocks(v))
    max_step = window // dilation
    qi = jnp.arange(A_BLOCK)[:, None]
    kj = jnp.arange(2 * A_BLOCK)[None, :]
    delta = qi + A_BLOCK - kj
    band = (delta >= 0) & (delta <= max_step)
    blk = jnp.arange(nb)[:, None, None]
    valid = band[None] & ((blk > 0) | (kj[None] >= A_BLOCK))
    bias = bias_table[:, rel_bucket(delta * dilation)]
    s = jnp.einsum('brnqhe,brnkhe->brnhqk', qs, kk) * (E ** -0.5) + bias
    s = jnp.where(valid[:, None], s, NEG_BIG)
    m = lax.stop_gradient(jnp.max(s, axis=-1, keepdims=True))
    pexp = jnp.exp(s - m)
    den = jnp.sum(pexp, axis=-1, keepdims=True)
    o = jnp.einsum('brnhqk,brnkhe->brnqhe', pexp / den, vv)
    lse = (m + jnp.log(den))[..., 0].transpose(0, 1, 2, 4, 3)

    def from_blocks(a):
        a = a.reshape((Bsz, dilation, nb * A_BLOCK) + a.shape[4:])
        a = jnp.swapaxes(a, 1, 2).reshape((Bsz, Tp) + a.shape[3:])
        return a[:, :T]

    return from_blocks(o), from_blocks(lse)


def dilated_attention(q, k, v, bias_table):
    outs, lses = [], []
    for window, dilation in A_PATTERNS:
        o, lse = dilated_branch(q, k, v, bias_table, window, dilation)
        outs.append(o)
        lses.append(lse)
    w = jax.nn.softmax(jnp.stack(lses, axis=0), axis=0)
    return jnp.einsum('gbth,gbthe->bthe', w, jnp.stack(outs, axis=0))


def hgrn2(q, z, i, g, lb, gnorm):
    Bsz, T, H, K = q.shape
    V = i.shape[-1]
    lb = lb.reshape(H, K)
    q = jax.nn.silu(q)
    sig = jax.nn.sigmoid(z)
    f = lb + (1.0 - lb) * sig
    log_f = jnp.log(jnp.maximum(f, 1e-30))
    kin = (1.0 - lb) * (1.0 - sig)
    Tp = -(-T // B_CHUNK) * B_CHUNK
    nc = Tp // B_CHUNK

    def chunks(a):
        a = jnp.pad(a, ((0, 0), (0, Tp - T), (0, 0), (0, 0)))
        return a.reshape(Bsz, nc, B_CHUNK, H, a.shape[-1]).transpose(1, 0, 3, 2, 4)

    causal = jnp.tril(jnp.ones((B_CHUNK, B_CHUNK), bool))[:, :, None]

    def step(S, xs):
        qc, kc, vc, gc = xs
        b = jnp.cumsum(gc, axis=2)
        diff = b[:, :, :, None, :] - b[:, :, None, :, :]
        decay = jnp.exp(jnp.where(causal, jnp.minimum(diff, 0.0), NEG_BIG))
        attn = jnp.einsum('bhik,bhjk,bhijk->bhij', qc, kc, decay)
        o = (jnp.einsum('bhij,bhjv->bhiv', attn, vc)
             + jnp.einsum('bhik,bhkv->bhiv', qc * jnp.exp(b), S))
        b_last = b[:, :, -1:, :]
        S = (jnp.exp(b_last[:, :, 0, :, None]) * S
             + jnp.einsum('bhjk,bhjv->bhkv', kc * jnp.exp(b_last - b), vc))
        return S, o

    S0 = jnp.zeros((Bsz, H, K, V), jnp.float32)
    _, o = lax.scan(step, S0, (chunks(q), chunks(kin), chunks(i), chunks(log_f)))
    o = o.transpose(1, 0, 3, 2, 4).reshape(Bsz, Tp, H, V)[:, :T]
    return rms_norm(o, gnorm) * jax.nn.silu(g)


def moba_attention(q, k, v, bias_table):
    Bsz, T, H, E = q.shape
    Tp = -(-T // C_BLOCK) * C_BLOCK
    nb = Tp // C_BLOCK
    topk = min(C_TOPK, nb)

    def heads_first(a):
        return jnp.pad(a, ((0, 0), (0, Tp - T), (0, 0), (0, 0))).transpose(0, 2, 1, 3)

    qh, kh, vh = heads_first(q), heads_first(k), heads_first(v)
    kb = kh.reshape(Bsz, H, nb, C_BLOCK, E)
    vb = vh.reshape(Bsz, H, nb, C_BLOCK, E)
    gate = jnp.einsum('bhte,bhne->bhtn', qh, jnp.mean(kb, axis=3))
    pos = jnp.arange(Tp)
    fully_past = jnp.arange(nb)[None, :] < (pos // C_BLOCK)[:, None]
    gate = jnp.where(fully_past, gate, NEG_BIG)
    gsel, idx = lax.top_k(gate, topk)
    sel_ok = gsel > 0.5 * NEG_BIG
    nq = Tp // C_QCHUNK

    def by_chunk(a):
        return a.reshape(Bsz, H, nq, C_QCHUNK, a.shape[-1]).transpose(2, 0, 1, 3, 4)

    bi = jnp.arange(Bsz)[:, None, None, None]
    hi = jnp.arange(H)[None, :, None, None]
    offs = jnp.arange(C_BLOCK)
    scale = E ** -0.5
    n_sel = topk * C_BLOCK

    def one_chunk(xs):
        qc, ic, okc, start = xs
        qpos = start + jnp.arange(C_QCHUNK)
        ksel = kb[bi, hi, ic]
        vsel = vb[bi, hi, ic]
        kpos = ic[..., None] * C_BLOCK + offs
        s_sel = (jnp.einsum('bhqe,bhqnle->bhqnl', qc, ksel) * scale
                 + bias_table[hi[..., None], rel_bucket(qpos[:, None, None] - kpos)])
        s_sel = jnp.where(okc[..., None], s_sel, NEG_BIG).reshape(Bsz, H, C_QCHUNK, n_sel)
        own = (start // C_BLOCK) * C_BLOCK
        kown = lax.dynamic_slice_in_dim(kh, own, C_BLOCK, axis=2)
        vown = lax.dynamic_slice_in_dim(vh, own, C_BLOCK, axis=2)
        rel = qpos[:, None] - (own + offs)[None, :]
        s_own = jnp.einsum('bhqe,bhle->bhql', qc, kown) * scale + bias_table[:, rel_bucket(rel)]
        s_own = jnp.where(rel >= 0, s_own, NEG_BIG)
        pr = jax.nn.softmax(jnp.concatenate([s_sel, s_own], axis=-1), axis=-1)
        return (jnp.einsum('bhqn,bhqne->bhqe', pr[..., :n_sel], vsel.reshape(Bsz, H, C_QCHUNK, n_sel, E))
                + jnp.einsum('bhql,bhle->bhqe', pr[..., n_sel:], vown))

    o = lax.map(one_chunk, (by_chunk(qh), by_chunk(idx), by_chunk(sel_ok), jnp.arange(nq) * C_QCHUNK))
    return o.transpose(1, 0, 3, 2, 4).reshape(Bsz, Tp, H, E)[:, :T]


def causal_dwconv(u, w, b):
    C = u.shape[-1]
    y = lax.conv_general_dilated(u, w[:, None, :].astype(u.dtype), window_strides=(1,),
                                 padding=((CONV_WIDTH - 1, 0),),
                                 dimension_numbers=('NWC', 'WIO', 'NWC'),
                                 feature_group_count=C)
    return y + b.astype(u.dtype)


def setup_inputs(seed: int = 0) -> dict:
    key = jax.random.key(seed)
    ks = jax.random.split(key, 17)
    f32 = jnp.float32

    def nrm(k, shape, scale):
        return jax.random.normal(k, shape, f32) * scale

    def gain(k, shape):
        return 1.0 + 0.02 * jax.random.normal(k, shape, f32)

    return {
        'x': nrm(ks[0], (BATCH, SEQ, D_MODEL), 1.0),
        'p': nrm(ks[1], (DEPTH, BATCH, SEQ, PLE_DIM), 1.0),
        'rel_bias': nrm(ks[2], (A_HEADS + C_HEADS, N_BUCKETS), 0.5),
        'norm_mix': gain(ks[3], (DEPTH, D_MODEL)),
        'w_in': nrm(ks[4], (DEPTH, D_MODEL, IN_WIDTH), D_MODEL ** -0.5),
        'w_out': nrm(ks[5], (DEPTH, MIX_WIDTH, D_MODEL), MIX_WIDTH ** -0.5),
        'lower_bounds': nrm(ks[6], (DEPTH, B_QK_WIDTH), 0.1),
        'hgrn_gnorm': gain(ks[7], (DEPTH, B_VAL_DIM)),
        'norm_ffn': gain(ks[8], (DEPTH, D_MODEL)),
        'w_up': nrm(ks[9], (DEPTH, D_MODEL, 2 * D_FF), D_MODEL ** -0.5),
        'conv_w': nrm(ks[10], (DEPTH, CONV_WIDTH, 2 * D_FF), CONV_WIDTH ** -0.5),
        'conv_b': nrm(ks[11], (DEPTH, 2 * D_FF), 0.01),
        'w_down': nrm(ks[12], (DEPTH, D_FF, D_MODEL), D_FF ** -0.5),
        'norm_ple': gain(ks[13], (DEPTH, D_MODEL)),
        'w_pe': nrm(ks[14], (DEPTH, PLE_DIM, D_MODEL), PLE_DIM ** -0.5),
        'w_pg': nrm(ks[15], (DEPTH, D_MODEL, D_MODEL), D_MODEL ** -0.5),
        'norm_final': gain(ks[16], (D_MODEL,)),
    }


def reference(x, p, rel_bias, norm_mix, w_in, w_out, lower_bounds, hgrn_gnorm, norm_ffn,
              w_up, conv_w, conv_b, w_down, norm_ple, w_pe, w_pg, norm_final):
    Bsz, T, _ = x.shape
    f32 = jnp.float32
    lb_sm = jax.nn.softmax(lower_bounds.astype(f32), axis=0)
    lb_all = jnp.cumsum(lb_sm, axis=0) - lb_sm[0]
    bias_a = rel_bias[:A_HEADS].astype(f32)
    bias_c = rel_bias[A_HEADS:].astype(f32)
    split_at = [int(s) for s in np.cumsum(IN_SPLIT_SIZES)[:-1]]

    for i in range(DEPTH):
        h = rms_norm(x, norm_mix[i])
        proj = (h @ w_in[i]).astype(f32)
        qa, ka, va, qb, fb, ib, gb, qc, kc, vc = jnp.split(proj, split_at, axis=-1)
        ya = dilated_attention(qa.reshape(Bsz, T, A_HEADS, HEAD_DIM), ka.reshape(Bsz, T, A_HEADS, HEAD_DIM),
                               va.reshape(Bsz, T, A_HEADS, HEAD_DIM), bias_a)
        yb = hgrn2(qb.reshape(Bsz, T, B_HEADS, B_KEY_DIM), fb.reshape(Bsz, T, B_HEADS, B_KEY_DIM),
                   ib.reshape(Bsz, T, B_HEADS, B_VAL_DIM), gb.reshape(Bsz, T, B_HEADS, B_VAL_DIM),
                   lb_all[i], hgrn_gnorm[i].astype(f32))
        yc = moba_attention(qc.reshape(Bsz, T, C_HEADS, HEAD_DIM), kc.reshape(Bsz, T, C_HEADS, HEAD_DIM),
                            vc.reshape(Bsz, T, C_HEADS, HEAD_DIM), bias_c)
        mix = jnp.concatenate([ya.reshape(Bsz, T, A_WIDTH), yb.reshape(Bsz, T, B_V_WIDTH),
                               yc.reshape(Bsz, T, C_WIDTH)], axis=-1).astype(x.dtype)
        x = x + mix @ w_out[i]
        h = rms_norm(x, norm_ffn[i])
        u = causal_dwconv(h @ w_up[i], conv_w[i], conv_b[i])
        ug, uv = jnp.split(u, 2, axis=-1)
        x = x + (jax.nn.silu(ug) * uv) @ w_down[i]
        gate = jax.nn.sigmoid(rms_norm(x, norm_ple[i]) @ w_pg[i])
        x = x + gate * (p[i].astype(x.dtype) @ w_pe[i])

    return rms_norm(x, norm_final)
```

```python
import functools
import math

import numpy as np
import jax
import jax.numpy as jnp
from jax import lax
from jax.experimental import pallas as pl
from jax.experimental.pallas import tpu as pltpu

F32 = jnp.float32
BF16 = jnp.bfloat16

HEAD_DIM = 64
A_HEADS = 6
A_PATTERNS = ((128, 1), (512, 4), (2048, 16))
A_BLOCK = 128
B_HEADS = 4
B_KEY_DIM = 128
B_VAL_DIM = 64
C_HEADS = 6
C_BLOCK = 256
C_TOPK = 3
N_BUCKETS = 32
MAX_DISTANCE = 2048
D_FF = 2816
PLE_DIM = 256
EPS = 1e-6
NEG_BIG = -1e30

A_WIDTH = A_HEADS * HEAD_DIM
B_QK_WIDTH = B_HEADS * B_KEY_DIM
B_V_WIDTH = B_HEADS * B_VAL_DIM
C_WIDTH = C_HEADS * HEAD_DIM
IN_WIDTH = 3 * A_WIDTH + 2 * B_QK_WIDTH + 2 * B_V_WIDTH + 3 * C_WIDTH
LANES = 128

B_GROUP_WIDTH = 2 * B_QK_WIDTH + 2 * B_V_WIDTH
A_COL0 = B_GROUP_WIDTH
C_COL0 = A_COL0 + 3 * A_WIDTH

HGRN_CHUNK = 64
HGRN_SUB = 16
VMEM_LIMIT = 56 * 1024 * 1024

_NT = (((1,), (1,)), ((), ()))


def _rel_bucket(dist):
    max_exact = N_BUCKETS // 2
    d = jnp.maximum(dist, 0)
    df = jnp.maximum(d, max_exact).astype(F32)
    large = max_exact + (jnp.log(df / max_exact) / math.log(MAX_DISTANCE / max_exact)
                         * (N_BUCKETS - max_exact)).astype(jnp.int32)
    large = jnp.minimum(large, N_BUCKETS - 1)
    return jnp.where(d < max_exact, d, large)


def _rel_bucket_np(dist):
    max_exact = N_BUCKETS // 2
    d = np.maximum(dist, 0)
    df = np.maximum(d, max_exact).astype(np.float64)
    large = max_exact + (np.log(df / max_exact) / math.log(MAX_DISTANCE / max_exact)
                         * (N_BUCKETS - max_exact)).astype(np.int64)
    large = np.minimum(large, N_BUCKETS - 1)
    return np.where(d < max_exact, d, large)


def _toeplitz(u, rows, cols):
    n = rows + cols - 1
    w = jnp.concatenate([u, jnp.zeros(u.shape[:-1] + (1,), u.dtype)], axis=-1)
    flat = jnp.tile(w, rows)[..., : rows * n]
    m = flat.reshape(u.shape[:-1] + (rows, n))
    return m[..., rows - 1: rows - 1 + cols]


def _rms(x, g):
    return x * lax.rsqrt(jnp.mean(x * x, axis=-1, keepdims=True) + EPS) * g


def _sigmoid(x):
    return 1.0 / (1.0 + jnp.exp(-x))


def _inproj_body(x_ref, g_ref, w_ref, o_ref, *, nch):
    h = _rms(x_ref[...], g_ref[...]).astype(BF16)
    for j in range(0, w_ref.shape[1], nch):
        o_ref[:, j:j + nch] = jnp.dot(h, w_ref[:, j:j + nch], preferred_element_type=F32)


def _inproj(x2, gain, w_bf, tm=512, nch=768):
    n, d = x2.shape
    width = w_bf.shape[1]
    return pl.pallas_call(
        functools.partial(_inproj_body, nch=nch),
        grid=(n // tm,),
        in_specs=[
            pl.BlockSpec((tm, d), lambda i: (i, 0)),
            pl.BlockSpec((1, d), lambda i: (0, 0)),
            pl.BlockSpec((d, width), lambda i: (0, 0), pipeline_mode=pl.Buffered(1)),
        ],
        out_specs=pl.BlockSpec((tm, width), lambda i: (i, 0)),
        out_shape=jax.ShapeDtypeStruct((n, width), F32),
        compiler_params=pltpu.CompilerParams(
            dimension_semantics=("arbitrary",), vmem_limit_bytes=VMEM_LIMIT),
        name="inproj",
    )(x2, gain, w_bf)


def _dilated_body(q_ref, kp_ref, k_ref, vp_ref, v_ref, bias_ref, o_ref, lse_ref, kbuf, vbuf,
                  *, group, max_step):
    L = A_BLOCK
    first = pl.program_id(2) == 0
    kbuf[0:L, :] = kp_ref[0].astype(BF16)
    kbuf[L:, :] = k_ref[0].astype(BF16)
    vbuf[0:L, :] = vp_ref[0].astype(BF16)
    vbuf[L:, :] = v_ref[0].astype(BF16)
    qi = lax.broadcasted_iota(jnp.int32, (L, 2 * L), 0)
    kj = lax.broadcasted_iota(jnp.int32, (L, 2 * L), 1)
    delta = qi + L - kj
    band = (delta >= 0) & (delta <= max_step)
    band_first = band & ((kj >= L) | jnp.logical_not(first))
    lo = lax.broadcasted_iota(jnp.int32, (L, LANES), 1) < HEAD_DIM
    scale = HEAD_DIM ** -0.5
    for g in range(group):
        valid = band_first if g == 0 else band
        for pair in range(A_HEADS // 2):
            cols = slice(pair * LANES, (pair + 1) * LANES)
            q2 = q_ref[0, g * L:(g + 1) * L, cols] * scale
            k2 = kbuf[g * L:(g + 2) * L, cols]
            v2 = vbuf[g * L:(g + 2) * L, cols]
            res = []
            for half in range(2):
                hmask = lo if half == 0 else jnp.logical_not(lo)
                qm = jnp.where(hmask, q2, 0.0).astype(BF16)
                s = lax.dot_general(qm, k2, _NT, preferred_element_type=F32)
                s = s + bias_ref[2 * pair + half]
                s = jnp.where(valid, s, NEG_BIG)
                m = jnp.max(s, axis=-1, keepdims=True)
                p = jnp.exp(s - m)
                den = jnp.sum(p, axis=-1, keepdims=True)
                o = jnp.dot(p.astype(BF16), v2, preferred_element_type=F32) / den
                res.append((o, m + jnp.log(den)))
            o_ref[0, g * L:(g + 1) * L, cols] = jnp.where(lo, res[0][0], res[1][0])
            lse_ref[0, g * L:(g + 1) * L, cols] = jnp.where(lo, res[0][1], res[1][1])


def _dilated_branch(proj, bias, window, dilation, group=4):
    bsz, t, _ = proj.shape
    L = A_BLOCK
    rows = t // dilation
    group = min(group, rows // L)
    assert rows % (L * group) == 0
    nsteps = rows // (L * group)
    pv = proj.reshape(bsz, rows, dilation * IN_WIDTH)
    ncol = IN_WIDTH // A_WIDTH
    qc, kc, vc = A_COL0 // A_WIDTH, A_COL0 // A_WIDTH + 1, A_COL0 // A_WIDTH + 2

    def own(c):
        return pl.BlockSpec((1, L * group, A_WIDTH), lambda b, r, n: (b, n, r * ncol + c))

    def prev(c):
        return pl.BlockSpec((1, L, A_WIDTH),
                            lambda b, r, n: (b, jnp.maximum(n * group - 1, 0), r * ncol + c))

    out_spec = pl.BlockSpec((1, L * group, A_WIDTH), lambda b, r, n: (b, n, r))
    o, lse = pl.pallas_call(
        functools.partial(_dilated_body, group=group, max_step=window // dilation),
        grid=(bsz, dilation, nsteps),
        in_specs=[own(qc), prev(kc), own(kc), prev(vc), own(vc),
                  pl.BlockSpec((A_HEADS, L, 2 * L), lambda b, r, n: (0, 0, 0))],
        out_specs=[out_spec, out_spec],
        out_shape=[jax.ShapeDtypeStruct((bsz, rows, dilation * A_WIDTH), F32)] * 2,
        scratch_shapes=[pltpu.VMEM((L * (group + 1), A_WIDTH), BF16),
                        pltpu.VMEM((L * (group + 1), A_WIDTH), BF16)],
        compiler_params=pltpu.CompilerParams(
            dimension_semantics=("arbitrary", "arbitrary", "arbitrary"),
            vmem_limit_bytes=VMEM_LIMIT),
        name=f"dilated_d{dilation}",
    )(pv, pv, pv, pv, pv, bias)
    return o.reshape(bsz, t, A_WIDTH), lse.reshape(bsz, t, A_WIDTH)


def _hgrn_body(q_ref, f_ref, i_ref, g_ref, lb_ref, gn_ref, o_ref, st_ref, *, chunk, sub, nchunk):
    C, c = chunk, sub
    K, V = B_KEY_DIM, B_VAL_DIM

    @pl.when(pl.program_id(1) == 0)
    def _():
        st_ref[...] = jnp.zeros_like(st_ref)

    ri = lax.broadcasted_iota(jnp.int32, (C, C), 0)
    ci = lax.broadcasted_iota(jnp.int32, (C, C), 1)
    tri = (ri >= ci).astype(F32)
    lag = jnp.where((ri // c == ci // c) & (ri >= ci), ri - ci, -1)

    def chunk_body(ic, carry):
        r0 = pl.multiple_of(ic * C, C)
        rows = pl.ds(r0, C)
        outs = []
        for h in range(B_HEADS):
            kc = slice(h * K, (h + 1) * K)
            vc = slice(h * V, (h + 1) * V)
            qraw = q_ref[0, rows, kc]
            z = f_ref[0, rows, kc]
            v = i_ref[0, rows, vc]
            gate = g_ref[0, rows, vc]
            lb = lb_ref[:, kc]
            sig = _sigmoid(z)
            fg = lb + (1.0 - lb) * sig
            g = jnp.log(jnp.maximum(fg, 1e-30))
            k = (1.0 - lb) * (1.0 - sig)
            q = qraw * _sigmoid(qraw)
            b = jnp.dot(tri, g, preferred_element_type=F32, precision=lax.Precision.HIGHEST)
            st = st_ref[h]
            v_bf = v.astype(BF16)
            o = lax.dot_general((q * jnp.exp(b)).astype(BF16), st.astype(BF16), _NT,
                                preferred_element_type=F32)
            amat = jnp.where(lag == 0, jnp.sum(q * k, axis=-1, keepdims=True), 0.0)
            for d in range(1, c):
                ks = pltpu.roll(k, d, 0)
                bs = pltpu.roll(b, d, 0)
                pr = q * ks * jnp.exp(jnp.minimum(b - bs, 0.0))
                amat = jnp.where(lag == d, jnp.sum(pr, axis=-1, keepdims=True), amat)
            for j in range(C // c - 1):
                e = c * (j + 1)
                r = b[e - 1:e]
                kh = (k[e - c:e] * jnp.exp(r - b[e - c:e])).astype(BF16)
                qt = (q[e:] * jnp.exp(b[e:] - r)).astype(BF16)
                a = lax.dot_general(qt, kh, _NT, preferred_element_type=F32)
                contrib = jnp.dot(a.astype(BF16), v_bf[e - c:e], preferred_element_type=F32)
                o = o + jnp.concatenate([jnp.zeros((e, V), F32), contrib], axis=0)
            o = o + jnp.dot(amat.astype(BF16), v_bf, preferred_element_type=F32)
            bl = b[C - 1:C]
            kd = (k * jnp.exp(bl - b)).astype(BF16)
            st_ref[h] = st * jnp.exp(bl) + jnp.dot(v.T.astype(BF16), kd, preferred_element_type=F32)
            on = _rms(o, gn_ref[...])
            outs.append(on * (gate * _sigmoid(gate)))
        o_ref[0, rows, :] = jnp.concatenate(outs, axis=-1)
        return carry

    lax.fori_loop(0, nchunk, chunk_body, 0)


def _hgrn(proj, lb, gnorm, tb=256):
    bsz, t, _ = proj.shape
    C = HGRN_CHUNK
    assert t % tb == 0 and tb % C == 0
    kw, vw = B_QK_WIDTH, B_V_WIDTH
    return pl.pallas_call(
        functools.partial(_hgrn_body, chunk=C, sub=HGRN_SUB, nchunk=tb // C),
        grid=(bsz, t // tb),
        in_specs=[
            pl.BlockSpec((1, tb, kw), lambda b, i: (b, i, 0)),
            pl.BlockSpec((1, tb, kw), lambda b, i: (b, i, 1)),
            pl.BlockSpec((1, tb, vw), lambda b, i: (b, i, 2 * kw // vw)),
            pl.BlockSpec((1, tb, vw), lambda b, i: (b, i, 2 * kw // vw + 1)),
            pl.BlockSpec((1, kw), lambda b, i: (0, 0)),
            pl.BlockSpec((1, B_VAL_DIM), lambda b, i: (0, 0)),
        ],
        out_specs=pl.BlockSpec((1, tb, vw), lambda b, i: (b, i, 0)),
        out_shape=jax.ShapeDtypeStruct((bsz, t, vw), F32),
        scratch_shapes=[pltpu.VMEM((B_HEADS, B_VAL_DIM, B_KEY_DIM), F32)],
        compiler_params=pltpu.CompilerParams(
            dimension_semantics=("arbitrary", "arbitrary"), vmem_limit_bytes=VMEM_LIMIT),
        name="hgrn2",
    )(proj, proj, proj, proj, lb, gnorm)


def _moba_body(q_ref, k_ref, v_ref, bias_ref, o_ref, kbf, vbf, kmean, acc, mref, lref,
               *, nb, ntile, topk):
    L = C_BLOCK
    qb = pl.program_id(2)

    @pl.when(qb == 0)
    def _():
        for n in range(nb):
            blk = k_ref[0, n * L:(n + 1) * L, :]
            kbf[n * L:(n + 1) * L, :] = blk.astype(BF16)
            kmean[n:n + 1, :] = jnp.mean(blk, axis=0, keepdims=True)
            vbf[n * L:(n + 1) * L, :] = v_ref[0, n * L:(n + 1) * L, :].astype(BF16)

    q2 = q_ref[0]
    lo = lax.broadcasted_iota(jnp.int32, (L, LANES), 1) < HEAD_DIM
    col = lax.broadcasted_iota(jnp.int32, (L, nb), 1)
    colf = col.astype(F32)
    ri = lax.broadcasted_iota(jnp.int32, (L, L), 0)
    ci = lax.broadcasted_iota(jnp.int32, (L, L), 1)
    causal = ri >= ci
    scale = HEAD_DIM ** -0.5
    own0 = pl.multiple_of(qb * L, L)
    kown = kbf[pl.ds(own0, L), :]
    vown = vbf[pl.ds(own0, L), :]

    qms, picks = [], []
    for half in range(2):
        hmask = lo if half == 0 else jnp.logical_not(lo)
        qf = jnp.where(hmask, q2, 0.0)
        gate = lax.dot_general(qf, kmean[...], _NT, preferred_element_type=F32,
                               precision=lax.Precision.HIGHEST)
        gate = jnp.where(col < qb, gate, NEG_BIG)
        sel = []
        for _ in range(topk):
            mx = jnp.max(gate, axis=-1, keepdims=True)
            ix = jnp.min(jnp.where(gate == mx, colf, float(nb)), axis=-1, keepdims=True)
            sel.append(jnp.where(mx > 0.5 * NEG_BIG, ix, -1.0))
            gate = jnp.where(colf == ix, -jnp.inf, gate)
        picks.append(sel)
        qm = (qf * scale).astype(BF16)
        qms.append(qm)
        s = lax.dot_general(qm, kown, _NT, preferred_element_type=F32) + bias_ref[half, 0]
        s = jnp.where(causal, s, NEG_BIG)
        m = jnp.max(s, axis=-1, keepdims=True)
        p = jnp.exp(s - m)
        mref[half] = m
        lref[half] = jnp.sum(p, axis=-1, keepdims=True)
        acc[half] = jnp.dot(p.astype(BF16), vown, preferred_element_type=F32)

    def body(kb, carry):
        k0 = pl.multiple_of(kb * L, L)
        kblk = kbf[pl.ds(k0, L), :]
        vblk = vbf[pl.ds(k0, L), :]
        tile = jnp.minimum(qb - kb, ntile - 1)
        kbv = kb.astype(F32)
        for half in range(2):
            sel = picks[half]
            hit = sel[0] == kbv
            for t in range(1, topk):
                hit = hit | (sel[t] == kbv)
            s = lax.dot_general(qms[half], kblk, _NT, preferred_element_type=F32) + bias_ref[half, tile]
            s = jnp.where(hit, s, NEG_BIG)
            m_old = mref[half]
            m_new = jnp.maximum(m_old, jnp.max(s, axis=-1, keepdims=True))
            alpha = jnp.exp(m_old - m_new)
            p = jnp.exp(s - m_new)
            lref[half] = alpha * lref[half] + jnp.sum(p, axis=-1, keepdims=True)
            acc[half] = alpha * acc[half] + jnp.dot(p.astype(BF16), vblk, preferred_element_type=F32)
            mref[half] = m_new
        return carry

    lax.fori_loop(0, qb, body, 0)
    o_ref[0] = jnp.where(lo, acc[0] / lref[0], acc[1] / lref[1])


def _moba(proj, bias_tiles):
    bsz, t, _ = proj.shape
    L = C_BLOCK
    assert t % L == 0
    nb = t // L
    assert nb % 8 == 0
    ntile = bias_tiles.shape[1]
    qcol, kcol, vcol = (C_COL0 // LANES, (C_COL0 + C_WIDTH) // LANES, (C_COL0 + 2 * C_WIDTH) // LANES)
    return pl.pallas_call(
        functools.partial(_moba_body, nb=nb, ntile=ntile, topk=min(C_TOPK, nb)),
        grid=(bsz, C_HEADS // 2, nb),
        in_specs=[
            pl.BlockSpec((1, L, LANES), lambda b, h, i: (b, i, qcol + h)),
            pl.BlockSpec((1, t, LANES), lambda b, h, i: (b, 0, kcol + h)),
            pl.BlockSpec((1, t, LANES), lambda b, h, i: (b, 0, vcol + h)),
            pl.BlockSpec((2, ntile, L, L), lambda b, h, i: (h, 0, 0, 0)),
        ],
        out_specs=pl.BlockSpec((1, L, LANES), lambda b, h, i: (b, i, h)),
        out_shape=jax.ShapeDtypeStruct((bsz, t, C_WIDTH), F32),
        scratch_shapes=[
            pltpu.VMEM((t, LANES), BF16),
            pltpu.VMEM((t, LANES), BF16),
            pltpu.VMEM((nb, LANES), F32),
            pltpu.VMEM((2, L, LANES), F32),
            pltpu.VMEM((2, L, 1), F32),
            pltpu.VMEM((2, L, 1), F32),
        ],
        compiler_params=pltpu.CompilerParams(
            dimension_semantics=("arbitrary", "arbitrary", "arbitrary"),
            vmem_limit_bytes=VMEM_LIMIT),
        name="moba",
    )(proj, proj, proj, bias_tiles)


def _post_body(x_ref, o1_ref, o2_ref, o3_ref, l1_ref, l2_ref, l3_ref, yb_ref, yc_ref, p_ref,
               wout_ref, nffn_ref, wup_ref, cw_ref, cb_ref, wdown_ref, nple_ref, wpg_ref, wpe_ref,
               nfin_ref, out_ref, carry_ref, *, final, fch):
    tm = x_ref.shape[1]

    @pl.when(pl.program_id(1) == 0)
    def _():
        carry_ref[...] = jnp.zeros_like(carry_ref)

    l1, l2, l3 = l1_ref[0], l2_ref[0], l3_ref[0]
    mx = jnp.maximum(jnp.maximum(l1, l2), l3)
    e1, e2, e3 = jnp.exp(l1 - mx), jnp.exp(l2 - mx), jnp.exp(l3 - mx)
    ya = (e1 * o1_ref[0] + e2 * o2_ref[0] + e3 * o3_ref[0]) / (e1 + e2 + e3)

    x = x_ref[0]
    x = x + jnp.dot(ya.astype(BF16), wout_ref[0:A_WIDTH, :], preferred_element_type=F32)
    x = x + jnp.dot(yb_ref[0].astype(BF16), wout_ref[A_WIDTH:A_WIDTH + B_V_WIDTH, :],
                    preferred_element_type=F32)
    x = x + jnp.dot(yc_ref[0].astype(BF16), wout_ref[A_WIDTH + B_V_WIDTH:, :],
                    preferred_element_type=F32)

    h = _rms(x, nffn_ref[...]).astype(BF16)
    row = lax.broadcasted_iota(jnp.int32, (tm, 1), 0)

    def conv_cols(c0):
        cols = slice(c0, c0 + fch)
        u = jnp.dot(h, wup_ref[:, cols], preferred_element_type=F32)
        prev = carry_ref[:, cols]
        carry_ref[:, cols] = u[tm - 8:tm]
        u1 = jnp.where(row == 0, prev[7:8], pltpu.roll(u, 1, 0))
        u2 = jnp.where(row == 0, prev[6:7], jnp.where(row == 1, prev[7:8], pltpu.roll(u, 2, 0)))
        w = cw_ref[:, cols]
        return w[0:1] * u2 + w[1:2] * u1 + w[2:3] * u + cb_ref[:, cols]

    acc = jnp.zeros(x.shape, F32)
    for j in range(D_FF // fch):
        ug = conv_cols(j * fch)
        uv = conv_cols(D_FF + j * fch)
        act = (ug * _sigmoid(ug)) * uv
        acc = acc + jnp.dot(act.astype(BF16), wdown_ref[j * fch:(j + 1) * fch, :],
                            preferred_element_type=F32)
    x = x + acc

    hp = _rms(x, nple_ref[...]).astype(BF16)
    gate = _sigmoid(jnp.dot(hp, wpg_ref[...], preferred_element_type=F32))
    x = x + gate * jnp.dot(p_ref[0].astype(BF16), wpe_ref[...], preferred_element_type=F32)
    if final:
        x = _rms(x, nfin_ref[...])
    out_ref[0] = x


def _post(x, oa, la, yb, yc, p, wout, nffn, wup, cw, cb, wdown, nple, wpg, wpe, nfin, final,
          tm=512, fch=256):
    bsz, t, d = x.shape
    assert t % tm == 0 and D_FF % fch == 0

    def tok(width):
        return pl.BlockSpec((1, tm, width), lambda b, i: (b, i, 0))

    def const(shape):
        return pl.BlockSpec(shape, lambda b, i: (0,) * len(shape), pipeline_mode=pl.Buffered(1))

    return pl.pallas_call(
        functools.partial(_post_body, final=final, fch=fch),
        grid=(bsz, t // tm),
        in_specs=[tok(d)] + [tok(A_WIDTH)] * 6 + [tok(B_V_WIDTH), tok(C_WIDTH), tok(PLE_DIM),
                  const(wout.shape), const(nffn.shape), const(wup.shape), const(cw.shape),
                  const(cb.shape), const(wdown.shape), const(nple.shape), const(wpg.shape),
                  const(wpe.shape), const(nfin.shape)],
        out_specs=tok(d),
        out_shape=jax.ShapeDtypeStruct((bsz, t, d), F32),
        scratch_shapes=[pltpu.VMEM((8, 2 * D_FF), F32)],
        compiler_params=pltpu.CompilerParams(
            dimension_semantics=("arbitrary", "arbitrary"), vmem_limit_bytes=VMEM_LIMIT),
        name="post",
    )(x, oa[0], oa[1], oa[2], la[0], la[1], la[2], yb, yc, p, wout, nffn, wup, cw, cb, wdown,
      nple, wpg, wpe, nfin)


def _dilated_bias(bias_a, dilation):
    L = A_BLOCK
    m = jnp.arange(3 * L - 1)
    u = bias_a[:, _rel_bucket((2 * L - 1 - m) * dilation)]
    return _toeplitz(u, L, 2 * L)


def _moba_bias(bias_c, t):
    L = C_BLOCK
    nb = t // L
    dist = np.arange(0, t)
    bk = _rel_bucket_np(dist)
    nfar = nb
    for dblk in range(nb, 0, -1):
        if np.all(bk[max(dblk * L - (L - 1), 0):] == bk[-1]):
            nfar = dblk
    m = jnp.arange(2 * L - 1)
    tiles = []
    for dblk in range(nfar):
        u = bias_c[:, _rel_bucket(dblk * L + (L - 1) - m)]
        tiles.append(_toeplitz(u, L, L))
    far = jnp.broadcast_to(bias_c[:, int(bk[-1])][:, None, None], (bias_c.shape[0], L, L))
    tiles.append(far)
    return jnp.stack(tiles, axis=1)


def kernel(x, p, rel_bias, norm_mix, w_in, w_out, lower_bounds, hgrn_gnorm, norm_ffn, w_up, conv_w,
           conv_b, w_down, norm_ple, w_pe, w_pg, norm_final):
    bsz, t, d = x.shape
    depth = w_in.shape[0]
    lb_sm = jax.nn.softmax(lower_bounds.astype(F32), axis=0)
    lb_all = jnp.cumsum(lb_sm, axis=0) - lb_sm[0]
    bias_a = rel_bias[:A_HEADS].astype(F32)
    bias_c = rel_bias[A_HEADS:].astype(F32)
    bias_a_tiles = [_dilated_bias(bias_a, dil) for _, dil in A_PATTERNS]
    bias_c_tiles = _moba_bias(bias_c, t)

    a_end = 3 * A_WIDTH
    b_end = a_end + B_GROUP_WIDTH
    for i in range(depth):
        w = w_in[i]
        w_perm = jnp.concatenate([w[:, a_end:b_end], w[:, :a_end], w[:, b_end:]], axis=1).astype(BF16)
        proj = _inproj(x.reshape(bsz * t, d), norm_mix[i][None], w_perm).reshape(bsz, t, IN_WIDTH)
        oa, la = [], []
        for (window, dil), bt in zip(A_PATTERNS, bias_a_tiles):
            o, lse = _dilated_branch(proj, bt, window, dil)
            oa.append(o)
            la.append(lse)
        yb = _hgrn(proj, lb_all[i][None], hgrn_gnorm[i].astype(F32)[None])
        yc = _moba(proj, bias_c_tiles)
        x = _post(x, oa, la, yb, yc, p[i], w_out[i].astype(BF16), norm_ffn[i][None],
                  w_up[i].astype(BF16), conv_w[i], conv_b[i][None], w_down[i].astype(BF16),
                  norm_ple[i][None], w_pg[i].astype(BF16), w_pe[i].astype(BF16), norm_final[None],
                  final=(i == depth - 1))
    return x
```

```python
import functools
import math

import numpy as np
import jax
import jax.numpy as jnp
from jax import lax
from jax.experimental import pallas as pl
from jax.experimental.pallas import tpu as pltpu

F32 = jnp.float32
BF16 = jnp.bfloat16

HEAD_DIM = 64
A_HEADS = 6
A_PATTERNS = ((128, 1), (512, 4), (2048, 16))
A_BLOCK = 128
B_HEADS = 4
B_KEY_DIM = 128
B_VAL_DIM = 64
C_HEADS = 6
C_BLOCK = 256
C_TOPK = 3
N_BUCKETS = 32
MAX_DISTANCE = 2048
D_FF = 2816
PLE_DIM = 256
EPS = 1e-6
NEG_BIG = -1e30

A_WIDTH = A_HEADS * HEAD_DIM
B_QK_WIDTH = B_HEADS * B_KEY_DIM
B_V_WIDTH = B_HEADS * B_VAL_DIM
C_WIDTH = C_HEADS * HEAD_DIM
IN_WIDTH = 3 * A_WIDTH + 2 * B_QK_WIDTH + 2 * B_V_WIDTH + 3 * C_WIDTH
LANES = 128

B_GROUP_WIDTH = 2 * B_QK_WIDTH + 2 * B_V_WIDTH
A_COL0 = B_GROUP_WIDTH
C_COL0 = A_COL0 + 3 * A_WIDTH

LOG2E = 1.4426950408889634
VT_ROWS = HEAD_DIM + 16

HGRN_CHUNK = 64
HGRN_SUB = 16
VMEM_LIMIT = 56 * 1024 * 1024

_NT = (((1,), (1,)), ((), ()))


def _rel_bucket(dist):
    max_exact = N_BUCKETS // 2
    d = jnp.maximum(dist, 0)
    df = jnp.maximum(d, max_exact).astype(F32)
    large = max_exact + (jnp.log(df / max_exact) / math.log(MAX_DISTANCE / max_exact)
                         * (N_BUCKETS - max_exact)).astype(jnp.int32)
    large = jnp.minimum(large, N_BUCKETS - 1)
    return jnp.where(d < max_exact, d, large)


def _rel_bucket_np(dist):
    max_exact = N_BUCKETS // 2
    d = np.maximum(dist, 0)
    df = np.maximum(d, max_exact).astype(np.float64)
    large = max_exact + (np.log(df / max_exact) / math.log(MAX_DISTANCE / max_exact)
                         * (N_BUCKETS - max_exact)).astype(np.int64)
    large = np.minimum(large, N_BUCKETS - 1)
    return np.where(d < max_exact, d, large)


def _toeplitz(u, rows, cols):
    n = rows + cols - 1
    w = jnp.concatenate([u, jnp.zeros(u.shape[:-1] + (1,), u.dtype)], axis=-1)
    flat = jnp.tile(w, rows)[..., : rows * n]
    m = flat.reshape(u.shape[:-1] + (rows, n))
    return m[..., rows - 1: rows - 1 + cols]


def _rms(x, g):
    return x * lax.rsqrt(jnp.mean(x * x, axis=-1, keepdims=True) + EPS) * g


def _sigmoid(x):
    return 1.0 / (1.0 + jnp.exp(-x))


def _inproj_body(x_ref, g_ref, w_ref, o_ref, *, nch):
    h = _rms(x_ref[...], g_ref[...]).astype(BF16)
    for j in range(0, w_ref.shape[1], nch):
        o_ref[:, j:j + nch] = jnp.dot(h, w_ref[:, j:j + nch], preferred_element_type=F32)


def _inproj(x2, gain, w_bf, tm=512, nch=768):
    n, d = x2.shape
    width = w_bf.shape[1]
    return pl.pallas_call(
        functools.partial(_inproj_body, nch=nch),
        grid=(n // tm,),
        in_specs=[
            pl.BlockSpec((tm, d), lambda i: (i, 0)),
            pl.BlockSpec((1, d), lambda i: (0, 0)),
            pl.BlockSpec((d, width), lambda i: (0, 0), pipeline_mode=pl.Buffered(1)),
        ],
        out_specs=pl.BlockSpec((tm, width), lambda i: (i, 0)),
        out_shape=jax.ShapeDtypeStruct((n, width), F32),
        compiler_params=pltpu.CompilerParams(
            dimension_semantics=("arbitrary",), vmem_limit_bytes=VMEM_LIMIT),
        name="inproj",
    )(x2, gain, w_bf)


def _dilated_body(q_ref, kp_ref, k_ref, vp_ref, v_ref, bias_ref, o_ref, lse_ref, kbuf, vbuf,
                  *, group, max_step):
    L = A_BLOCK
    first = pl.program_id(2) == 0
    kbuf[0:L, :] = kp_ref[0].astype(BF16)
    kbuf[L:, :] = k_ref[0].astype(BF16)
    vbuf[0:L, :] = vp_ref[0].astype(BF16)
    vbuf[L:, :] = v_ref[0].astype(BF16)
    qi = lax.broadcasted_iota(jnp.int32, (L, 2 * L), 0)
    kj = lax.broadcasted_iota(jnp.int32, (L, 2 * L), 1)
    delta = qi + L - kj
    band = (delta >= 0) & (delta <= max_step)
    band_first = band & ((kj >= L) | jnp.logical_not(first))
    lo = lax.broadcasted_iota(jnp.int32, (L, LANES), 1) < HEAD_DIM
    scale = HEAD_DIM ** -0.5
    for g in range(group):
        valid = band_first if g == 0 else band
        for pair in range(A_HEADS // 2):
            cols = slice(pair * LANES, (pair + 1) * LANES)
            q2 = q_ref[0, g * L:(g + 1) * L, cols] * scale
            k2 = kbuf[g * L:(g + 2) * L, cols]
            v2 = vbuf[g * L:(g + 2) * L, cols]
            res = []
            for half in range(2):
                hmask = lo if half == 0 else jnp.logical_not(lo)
                qm = jnp.where(hmask, q2, 0.0).astype(BF16)
                s = lax.dot_general(qm, k2, _NT, preferred_element_type=F32)
                s = s + bias_ref[2 * pair + half]
                s = jnp.where(valid, s, NEG_BIG)
                m = jnp.max(s, axis=-1, keepdims=True)
                p = jnp.exp(s - m)
                den = jnp.sum(p, axis=-1, keepdims=True)
                o = jnp.dot(p.astype(BF16), v2, preferred_element_type=F32) / den
                res.append((o, m + jnp.log(den)))
            o_ref[0, g * L:(g + 1) * L, cols] = jnp.where(lo, res[0][0], res[1][0])
            lse_ref[0, g * L:(g + 1) * L, cols] = jnp.where(lo, res[0][1], res[1][1])


def _dilated_branch(proj, bias, window, dilation, group=4):
    bsz, t, _ = proj.shape
    L = A_BLOCK
    rows = t // dilation
    group = min(group, rows // L)
    assert rows % (L * group) == 0
    nsteps = rows // (L * group)
    pv = proj.reshape(bsz, rows, dilation * IN_WIDTH)
    ncol = IN_WIDTH // A_WIDTH
    qc, kc, vc = A_COL0 // A_WIDTH, A_COL0 // A_WIDTH + 1, A_COL0 // A_WIDTH + 2

    def own(c):
        return pl.BlockSpec((1, L * group, A_WIDTH), lambda b, r, n: (b, n, r * ncol + c))

    def prev(c):
        return pl.BlockSpec((1, L, A_WIDTH),
                            lambda b, r, n: (b, jnp.maximum(n * group - 1, 0), r * ncol + c))

    out_spec = pl.BlockSpec((1, L * group, A_WIDTH), lambda b, r, n: (b, n, r))
    o, lse = pl.pallas_call(
        functools.partial(_dilated_body, group=group, max_step=window // dilation),
        grid=(bsz, dilation, nsteps),
        in_specs=[own(qc), prev(kc), own(kc), prev(vc), own(vc),
                  pl.BlockSpec((A_HEADS, L, 2 * L), lambda b, r, n: (0, 0, 0))],
        out_specs=[out_spec, out_spec],
        out_shape=[jax.ShapeDtypeStruct((bsz, rows, dilation * A_WIDTH), F32)] * 2,
        scratch_shapes=[pltpu.VMEM((L * (group + 1), A_WIDTH), BF16),
                        pltpu.VMEM((L * (group + 1), A_WIDTH), BF16)],
        compiler_params=pltpu.CompilerParams(
            dimension_semantics=("arbitrary", "arbitrary", "arbitrary"),
            vmem_limit_bytes=VMEM_LIMIT),
        name=f"dilated_d{dilation}",
    )(pv, pv, pv, pv, pv, bias)
    return o.reshape(bsz, t, A_WIDTH), lse.reshape(bsz, t, A_WIDTH)


def _hgrn_body(q_ref, f_ref, i_ref, g_ref, lb_ref, gn_ref, o_ref, st_ref, *, chunk, sub, nchunk):
    C, c = chunk, sub
    K, V = B_KEY_DIM, B_VAL_DIM

    @pl.when(pl.program_id(1) == 0)
    def _():
        st_ref[...] = jnp.zeros_like(st_ref)

    ri = lax.broadcasted_iota(jnp.int32, (C, C), 0)
    ci = lax.broadcasted_iota(jnp.int32, (C, C), 1)
    tri = (ri >= ci).astype(F32)
    lag = jnp.where((ri // c == ci // c) & (ri >= ci), ri - ci, -1)

    def chunk_body(ic, carry):
        r0 = pl.multiple_of(ic * C, C)
        rows = pl.ds(r0, C)
        outs = []
        for h in range(B_HEADS):
            kc = slice(h * K, (h + 1) * K)
            vc = slice(h * V, (h + 1) * V)
            qraw = q_ref[0, rows, kc]
            z = f_ref[0, rows, kc]
            v = i_ref[0, rows, vc]
            gate = g_ref[0, rows, vc]
            lb = lb_ref[:, kc]
            sig = _sigmoid(z)
            fg = lb + (1.0 - lb) * sig
            g = jnp.log(jnp.maximum(fg, 1e-30))
            k = (1.0 - lb) * (1.0 - sig)
            q = qraw * _sigmoid(qraw)
            b = jnp.dot(tri, g, preferred_element_type=F32, precision=lax.Precision.HIGHEST)
            st = st_ref[h]
            v_bf = v.astype(BF16)
            o = lax.dot_general((q * jnp.exp(b)).astype(BF16), st.astype(BF16), _NT,
                                preferred_element_type=F32)
            amat = jnp.where(lag == 0, jnp.sum(q * k, axis=-1, keepdims=True), 0.0)
            for d in range(1, c):
                ks = pltpu.roll(k, d, 0)
                bs = pltpu.roll(b, d, 0)
                pr = q * ks * jnp.exp(jnp.minimum(b - bs, 0.0))
                amat = jnp.where(lag == d, jnp.sum(pr, axis=-1, keepdims=True), amat)
            for j in range(C // c - 1):
                e = c * (j + 1)
                r = b[e - 1:e]
                kh = (k[e - c:e] * jnp.exp(r - b[e - c:e])).astype(BF16)
                qt = (q[e:] * jnp.exp(b[e:] - r)).astype(BF16)
                a = lax.dot_general(qt, kh, _NT, preferred_element_type=F32)
                contrib = jnp.dot(a.astype(BF16), v_bf[e - c:e], preferred_element_type=F32)
                o = o + jnp.concatenate([jnp.zeros((e, V), F32), contrib], axis=0)
            o = o + jnp.dot(amat.astype(BF16), v_bf, preferred_element_type=F32)
            bl = b[C - 1:C]
            kd = (k * jnp.exp(bl - b)).astype(BF16)
            st_ref[h] = st * jnp.exp(bl) + jnp.dot(v.T.astype(BF16), kd, preferred_element_type=F32)
            on = _rms(o, gn_ref[...])
            outs.append(on * (gate * _sigmoid(gate)))
        o_ref[0, rows, :] = jnp.concatenate(outs, axis=-1)
        return carry

    lax.fori_loop(0, nchunk, chunk_body, 0)


def _hgrn(proj, lb, gnorm, tb=256):
    bsz, t, _ = proj.shape
    C = HGRN_CHUNK
    assert t % tb == 0 and tb % C == 0
    kw, vw = B_QK_WIDTH, B_V_WIDTH
    return pl.pallas_call(
        functools.partial(_hgrn_body, chunk=C, sub=HGRN_SUB, nchunk=tb // C),
        grid=(bsz, t // tb),
        in_specs=[
            pl.BlockSpec((1, tb, kw), lambda b, i: (b, i, 0)),
            pl.BlockSpec((1, tb, kw), lambda b, i: (b, i, 1)),
            pl.BlockSpec((1, tb, vw), lambda b, i: (b, i, 2 * kw // vw)),
            pl.BlockSpec((1, tb, vw), lambda b, i: (b, i, 2 * kw // vw + 1)),
            pl.BlockSpec((1, kw), lambda b, i: (0, 0)),
            pl.BlockSpec((1, B_VAL_DIM), lambda b, i: (0, 0)),
        ],
        out_specs=pl.BlockSpec((1, tb, vw), lambda b, i: (b, i, 0)),
        out_shape=jax.ShapeDtypeStruct((bsz, t, vw), F32),
        scratch_shapes=[pltpu.VMEM((B_HEADS, B_VAL_DIM, B_KEY_DIM), F32)],
        compiler_params=pltpu.CompilerParams(
            dimension_semantics=("arbitrary", "arbitrary"), vmem_limit_bytes=VMEM_LIMIT),
        name="hgrn2",
    )(proj, proj, proj, proj, lb, gnorm)


def _moba_body(q_ref, k_ref, v_ref, bias_ref, o_ref, kbh, vtb, kmean, qx, sbuf, pbuf, acc,
               *, nb, ntile, topk, unroll):
    L = C_BLOCK
    E = HEAD_DIM
    G = LANES // 4
    U = unroll
    qb = pl.program_id(2)
    lane = lax.broadcasted_iota(jnp.int32, (L, LANES), 1)

    @pl.when(qb == 0)
    def _():
        kmean[...] = jnp.zeros_like(kmean)
        ones_rows = jnp.ones((VT_ROWS - E, L), BF16)
        for n in range(nb):
            blk = k_ref[0, n * L:(n + 1) * L, :]
            mean = jnp.mean(blk, axis=0, keepdims=True)
            for g in range(LANES // G):
                kmean[g * G + n:g * G + n + 1, :] = mean
            kbh[n, 0] = jnp.where(lane < E, blk, (lane == 2 * G + n).astype(F32)).astype(BF16)
            kbh[n, 1] = jnp.where(lane >= E, blk, (lane == n).astype(F32)).astype(BF16)
            vt = v_ref[0, n * L:(n + 1) * L, :].T.astype(BF16)
            for half in range(2):
                vtb[n, half, 0:E, :] = vt[half * E:(half + 1) * E, :]
                vtb[n, half, E:, :] = ones_rows
        kbh[nb, 0] = (lane == 3 * G).astype(BF16)
        kbh[nb, 1] = (lane == G).astype(BF16)

    q2 = q_ref[0]
    slot = lane % G
    grp = lane // G
    slot_f = slot.astype(F32)
    qscale = (E ** -0.5) * LOG2E

    for half in range(2):
        hmask = (lane < E) if half == 0 else (lane >= E)
        qf = jnp.where(hmask, q2, 0.0)
        gate = lax.dot_general(qf, kmean[...], _NT, preferred_element_type=F32,
                               precision=lax.Precision.HIGHEST)
        gate = jnp.where(slot < qb, gate, NEG_BIG)
        picked = jnp.zeros((L, LANES), jnp.bool_)
        for _ in range(topk):
            mx = jnp.max(gate, axis=-1, keepdims=True)
            ix = jnp.min(jnp.where(gate == mx, slot_f, float(G)), axis=-1, keepdims=True)
            hit = slot_f == ix
            picked = picked | (hit & (mx > 0.5 * NEG_BIG))
            gate = jnp.where(hit, -jnp.inf, gate)
        wanted = picked | (slot == qb)
        sel_grp = 2 if half == 0 else 0
        ext = jnp.where((grp == sel_grp + 1) | ((grp == sel_grp) & jnp.logical_not(wanted)), NEG_BIG, 0.0)
        qx[half] = jnp.where(hmask, qf * qscale, ext).astype(BF16)

    def scores_stage(dist, slot_):
        kb = jnp.where(dist <= qb, qb - dist, nb)
        tile = jnp.minimum(dist, ntile - 1)
        for half in range(2):
            s = lax.dot_general(kbh[kb, half], qx[half], _NT, preferred_element_type=F32)
            sbuf[slot_, half] = s + bias_ref[half, tile]

    def softmax_stage(slot_, m_old):
        out = []
        for half in range(2):
            s = sbuf[slot_, half]
            m_new = jnp.maximum(m_old[half], jnp.max(s, axis=0, keepdims=True))
            pbuf[slot_, half] = jnp.exp2(s - m_new).astype(BF16)
            out.append(m_new)
        return tuple(out)

    def pv_stage(dist, slot_, alpha):
        kb = jnp.clip(qb - dist, 0, nb - 1)
        for half in range(2):
            acc[half] = alpha[half] * acc[half] + jnp.dot(vtb[kb, half], pbuf[slot_, half],
                                                          preferred_element_type=F32)

    pbuf[...] = jnp.zeros_like(pbuf)
    acc[...] = jnp.zeros_like(acc)
    scores_stage(0, 0)

    def trip(j, carry):
        m_old, alpha = carry[0:2], carry[2:4]
        for u in range(U):
            dist = j * U + u
            pv_stage(dist - 1, (u - 1) % U, alpha)
            m_new = softmax_stage(u, m_old)
            scores_stage(dist + 1, (u + 1) % U)
            alpha = tuple(jnp.exp2(a - b) for a, b in zip(m_old, m_new))
            m_old = m_new
        return m_old + alpha

    row = jnp.full((1, L), NEG_BIG, F32)
    one = jnp.ones((1, L), F32)
    ntrips = (qb + U) // U
    carry = lax.fori_loop(0, ntrips, trip, (row, row, one, one))
    pv_stage(ntrips * U - 1, (U - 1) % U, carry[2:4])
    out_t = jnp.concatenate([acc[half, 0:E, :] / acc[half, E:E + 1, :] for half in range(2)], axis=0)
    o_ref[0] = out_t.T


def _moba(proj, bias_tiles, unroll=2):
    bsz, t, _ = proj.shape
    L = C_BLOCK
    assert t % L == 0
    nb = t // L
    assert nb <= LANES // 4, "one selection lane per key block and lane group"
    ntile = bias_tiles.shape[1]
    qcol, kcol, vcol = (C_COL0 // LANES, (C_COL0 + C_WIDTH) // LANES, (C_COL0 + 2 * C_WIDTH) // LANES)
    return pl.pallas_call(
        functools.partial(_moba_body, nb=nb, ntile=ntile, topk=min(C_TOPK, nb), unroll=unroll),
        grid=(bsz, C_HEADS // 2, nb),
        in_specs=[
            pl.BlockSpec((1, L, LANES), lambda b, h, i: (b, i, qcol + h)),
            pl.BlockSpec((1, t, LANES), lambda b, h, i: (b, 0, kcol + h)),
            pl.BlockSpec((1, t, LANES), lambda b, h, i: (b, 0, vcol + h)),
            pl.BlockSpec((2, ntile, L, L), lambda b, h, i: (h, 0, 0, 0)),
        ],
        out_specs=pl.BlockSpec((1, L, LANES), lambda b, h, i: (b, i, h)),
        out_shape=jax.ShapeDtypeStruct((bsz, t, C_WIDTH), F32),
        scratch_shapes=[
            pltpu.VMEM((nb + 1, 2, L, LANES), BF16),
            pltpu.VMEM((nb, 2, VT_ROWS, L), BF16),
            pltpu.VMEM((LANES, LANES), F32),
            pltpu.VMEM((2, L, LANES), BF16),
            pltpu.VMEM((unroll, 2, L, L), F32),
            pltpu.VMEM((unroll, 2, L, L), BF16),
            pltpu.VMEM((2, VT_ROWS, L), F32),
        ],
        compiler_params=pltpu.CompilerParams(
            dimension_semantics=("arbitrary", "arbitrary", "arbitrary"),
            vmem_limit_bytes=VMEM_LIMIT),
        name="moba",
    )(proj, proj, proj, bias_tiles)


def _post_body(x_ref, o1_ref, o2_ref, o3_ref, l1_ref, l2_ref, l3_ref, yb_ref, yc_ref, p_ref,
               wout_ref, nffn_ref, wup_ref, cw_ref, cb_ref, wdown_ref, nple_ref, wpg_ref, wpe_ref,
               nfin_ref, out_ref, carry_ref, *, final, fch):
    tm = x_ref.shape[1]

    @pl.when(pl.program_id(1) == 0)
    def _():
        carry_ref[...] = jnp.zeros_like(carry_ref)

    l1, l2, l3 = l1_ref[0], l2_ref[0], l3_ref[0]
    mx = jnp.maximum(jnp.maximum(l1, l2), l3)
    e1, e2, e3 = jnp.exp(l1 - mx), jnp.exp(l2 - mx), jnp.exp(l3 - mx)
    ya = (e1 * o1_ref[0] + e2 * o2_ref[0] + e3 * o3_ref[0]) / (e1 + e2 + e3)

    x = x_ref[0]
    x = x + jnp.dot(ya.astype(BF16), wout_ref[0:A_WIDTH, :], preferred_element_type=F32)
    x = x + jnp.dot(yb_ref[0].astype(BF16), wout_ref[A_WIDTH:A_WIDTH + B_V_WIDTH, :],
                    preferred_element_type=F32)
    x = x + jnp.dot(yc_ref[0].astype(BF16), wout_ref[A_WIDTH + B_V_WIDTH:, :],
                    preferred_element_type=F32)

    h = _rms(x, nffn_ref[...]).astype(BF16)
    row = lax.broadcasted_iota(jnp.int32, (tm, 1), 0)

    def conv_cols(c0):
        cols = slice(c0, c0 + fch)
        u = jnp.dot(h, wup_ref[:, cols], preferred_element_type=F32)
        prev = carry_ref[:, cols]
        carry_ref[:, cols] = u[tm - 8:tm]
        u1 = jnp.where(row == 0, prev[7:8], pltpu.roll(u, 1, 0))
        u2 = jnp.where(row == 0, prev[6:7], jnp.where(row == 1, prev[7:8], pltpu.roll(u, 2, 0)))
        w = cw_ref[:, cols]
        return w[0:1] * u2 + w[1:2] * u1 + w[2:3] * u + cb_ref[:, cols]

    acc = jnp.zeros(x.shape, F32)
    for j in range(D_FF // fch):
        ug = conv_cols(j * fch)
        uv = conv_cols(D_FF + j * fch)
        act = (ug * _sigmoid(ug)) * uv
        acc = acc + jnp.dot(act.astype(BF16), wdown_ref[j * fch:(j + 1) * fch, :],
                            preferred_element_type=F32)
    x = x + acc

    hp = _rms(x, nple_ref[...]).astype(BF16)
    gate = _sigmoid(jnp.dot(hp, wpg_ref[...], preferred_element_type=F32))
    x = x + gate * jnp.dot(p_ref[0].astype(BF16), wpe_ref[...], preferred_element_type=F32)
    if final:
        x = _rms(x, nfin_ref[...])
    out_ref[0] = x


def _post(x, oa, la, yb, yc, p, wout, nffn, wup, cw, cb, wdown, nple, wpg, wpe, nfin, final,
          tm=512, fch=256):
    bsz, t, d = x.shape
    assert t % tm == 0 and D_FF % fch == 0

    def tok(width):
        return pl.BlockSpec((1, tm, width), lambda b, i: (b, i, 0))

    def const(shape):
        return pl.BlockSpec(shape, lambda b, i: (0,) * len(shape), pipeline_mode=pl.Buffered(1))

    return pl.pallas_call(
        functools.partial(_post_body, final=final, fch=fch),
        grid=(bsz, t // tm),
        in_specs=[tok(d)] + [tok(A_WIDTH)] * 6 + [tok(B_V_WIDTH), tok(C_WIDTH), tok(PLE_DIM),
                  const(wout.shape), const(nffn.shape), const(wup.shape), const(cw.shape),
                  const(cb.shape), const(wdown.shape), const(nple.shape), const(wpg.shape),
                  const(wpe.shape), const(nfin.shape)],
        out_specs=tok(d),
        out_shape=jax.ShapeDtypeStruct((bsz, t, d), F32),
        scratch_shapes=[pltpu.VMEM((8, 2 * D_FF), F32)],
        compiler_params=pltpu.CompilerParams(
            dimension_semantics=("arbitrary", "arbitrary"), vmem_limit_bytes=VMEM_LIMIT),
        name="post",
    )(x, oa[0], oa[1], oa[2], la[0], la[1], la[2], yb, yc, p, wout, nffn, wup, cw, cb, wdown,
      nple, wpg, wpe, nfin)


def _dilated_bias(bias_a, dilation):
    L = A_BLOCK
    m = jnp.arange(3 * L - 1)
    u = bias_a[:, _rel_bucket((2 * L - 1 - m) * dilation)]
    return _toeplitz(u, L, 2 * L)


def _moba_bias(bias_c, t):
    L = C_BLOCK
    nb = t // L
    dist = np.arange(0, t)
    bk = _rel_bucket_np(dist)
    nfar = nb
    for dblk in range(nb, 0, -1):
        if np.all(bk[max(dblk * L - (L - 1), 0):] == bk[-1]):
            nfar = dblk
    m = jnp.arange(2 * L - 1)
    tiles = []
    for dblk in range(nfar):
        u = bias_c[:, _rel_bucket(dblk * L + m - (L - 1))]
        tiles.append(_toeplitz(u, L, L))
    far = jnp.broadcast_to(bias_c[:, int(bk[-1])][:, None, None], (bias_c.shape[0], L, L))
    tiles.append(far)
    kj = jnp.arange(L)[:, None]
    qi = jnp.arange(L)[None, :]
    tiles[0] = jnp.where(qi >= kj, tiles[0] * LOG2E, NEG_BIG)
    tiles[1:] = [tl * LOG2E for tl in tiles[1:]]
    return jnp.stack(tiles, axis=1)


def kernel(x, p, rel_bias, norm_mix, w_in, w_out, lower_bounds, hgrn_gnorm, norm_ffn, w_up, conv_w,
           conv_b, w_down, norm_ple, w_pe, w_pg, norm_final):
    bsz, t, d = x.shape
    depth = w_in.shape[0]
    lb_sm = jax.nn.softmax(lower_bounds.astype(F32), axis=0)
    lb_all = jnp.cumsum(lb_sm, axis=0) - lb_sm[0]
    bias_a = rel_bias[:A_HEADS].astype(F32)
    bias_c = rel_bias[A_HEADS:].astype(F32)
    bias_a_tiles = [_dilated_bias(bias_a, dil) for _, dil in A_PATTERNS]
    bias_c_tiles = _moba_bias(bias_c, t)

    a_end = 3 * A_WIDTH
    b_end = a_end + B_GROUP_WIDTH
    for i in range(depth):
        w = w_in[i]
        w_perm = jnp.concatenate([w[:, a_end:b_end], w[:, :a_end], w[:, b_end:]], axis=1).astype(BF16)
        proj = _inproj(x.reshape(bsz * t, d), norm_mix[i][None], w_perm).reshape(bsz, t, IN_WIDTH)
        oa, la = [], []
        for (window, dil), bt in zip(A_PATTERNS, bias_a_tiles):
            o, lse = _dilated_branch(proj, bt, window, dil)
            oa.append(o)
            la.append(lse)
        yb = _hgrn(proj, lb_all[i][None], hgrn_gnorm[i].astype(F32)[None])
        yc = _moba(proj, bias_c_tiles)
        x = _post(x, oa, la, yb, yc, p[i], w_out[i].astype(BF16), norm_ffn[i][None],
                  w_up[i].astype(BF16), conv_w[i], conv_b[i][None], w_down[i].astype(BF16),
                  norm_ple[i][None], w_pg[i].astype(BF16), w_pe[i].astype(BF16), norm_final[None],
                  final=(i == depth - 1))
    return x
```

```python
import functools
import math

import numpy as np
import jax
import jax.numpy as jnp
from jax import lax
from jax.experimental import pallas as pl
from jax.experimental.pallas import tpu as pltpu

F32 = jnp.float32
BF16 = jnp.bfloat16

HEAD_DIM = 64
A_HEADS = 6
A_PATTERNS = ((128, 1), (512, 4), (2048, 16))
A_BLOCK = 128
B_HEADS = 4
B_KEY_DIM = 128
B_VAL_DIM = 64
C_HEADS = 6
C_BLOCK = 256
C_TOPK = 3
N_BUCKETS = 32
MAX_DISTANCE = 2048
D_FF = 2816
PLE_DIM = 256
EPS = 1e-6
NEG_BIG = -1e30

A_WIDTH = A_HEADS * HEAD_DIM
B_QK_WIDTH = B_HEADS * B_KEY_DIM
B_V_WIDTH = B_HEADS * B_VAL_DIM
C_WIDTH = C_HEADS * HEAD_DIM
IN_WIDTH = 3 * A_WIDTH + 2 * B_QK_WIDTH + 2 * B_V_WIDTH + 3 * C_WIDTH
LANES = 128

B_GROUP_WIDTH = 2 * B_QK_WIDTH + 2 * B_V_WIDTH
A_COL0 = B_GROUP_WIDTH
C_COL0 = A_COL0 + 3 * A_WIDTH

LOG2E = 1.4426950408889634
VT_ROWS = HEAD_DIM + 16

HGRN_CHUNK = 64
HGRN_SUB = 8
HGRN_FSUB = 16
HGRN_FACTOR_LIMIT = 64.0
VMEM_LIMIT = 56 * 1024 * 1024

_NT = (((1,), (1,)), ((), ()))


def _rel_bucket(dist):
    max_exact = N_BUCKETS // 2
    d = jnp.maximum(dist, 0)
    df = jnp.maximum(d, max_exact).astype(F32)
    large = max_exact + (jnp.log(df / max_exact) / math.log(MAX_DISTANCE / max_exact)
                         * (N_BUCKETS - max_exact)).astype(jnp.int32)
    large = jnp.minimum(large, N_BUCKETS - 1)
    return jnp.where(d < max_exact, d, large)


def _rel_bucket_np(dist):
    max_exact = N_BUCKETS // 2
    d = np.maximum(dist, 0)
    df = np.maximum(d, max_exact).astype(np.float64)
    large = max_exact + (np.log(df / max_exact) / math.log(MAX_DISTANCE / max_exact)
                         * (N_BUCKETS - max_exact)).astype(np.int64)
    large = np.minimum(large, N_BUCKETS - 1)
    return np.where(d < max_exact, d, large)


def _toeplitz(u, rows, cols):
    n = rows + cols - 1
    w = jnp.concatenate([u, jnp.zeros(u.shape[:-1] + (1,), u.dtype)], axis=-1)
    flat = jnp.tile(w, rows)[..., : rows * n]
    m = flat.reshape(u.shape[:-1] + (rows, n))
    return m[..., rows - 1: rows - 1 + cols]


def _rms(x, g):
    return x * lax.rsqrt(jnp.mean(x * x, axis=-1, keepdims=True) + EPS) * g


def _sigmoid(x):
    return 1.0 / (1.0 + jnp.exp(-x))


def _inproj_body(x_ref, g_ref, w_ref, o_ref, *, nch):
    h = _rms(x_ref[...], g_ref[...]).astype(BF16)
    for j in range(0, w_ref.shape[1], nch):
        o_ref[:, j:j + nch] = jnp.dot(h, w_ref[:, j:j + nch], preferred_element_type=F32)


def _inproj(x2, gain, w_bf, tm=512, nch=768):
    n, d = x2.shape
    width = w_bf.shape[1]
    return pl.pallas_call(
        functools.partial(_inproj_body, nch=nch),
        grid=(n // tm,),
        in_specs=[
            pl.BlockSpec((tm, d), lambda i: (i, 0)),
            pl.BlockSpec((1, d), lambda i: (0, 0)),
            pl.BlockSpec((d, width), lambda i: (0, 0), pipeline_mode=pl.Buffered(1)),
        ],
        out_specs=pl.BlockSpec((tm, width), lambda i: (i, 0)),
        out_shape=jax.ShapeDtypeStruct((n, width), F32),
        compiler_params=pltpu.CompilerParams(
            dimension_semantics=("arbitrary",), vmem_limit_bytes=VMEM_LIMIT),
        name="inproj",
    )(x2, gain, w_bf)


def _dilated_body(q_ref, kp_ref, k_ref, vp_ref, v_ref, bias_ref, o_ref, m_sc, l_sc, acc_sc,
                  *, span, patterns, unroll):
    L = A_BLOCK
    first_span = pl.program_id(2) == 0
    qi = lax.broadcasted_iota(jnp.int32, (L, 2 * L), 0)
    kj = lax.broadcasted_iota(jnp.int32, (L, 2 * L), 1)
    delta = qi + L - kj
    lo = lax.broadcasted_iota(jnp.int32, (L, LANES), 1) < HEAD_DIM
    scale = HEAD_DIM ** -0.5
    last = len(patterns) - 1

    for pi, (window, d) in enumerate(patterns):
        band = (delta >= 0) & (delta <= window // d)
        nblk = span // (L * d)

        def rows_at(start, d=d):
            return pl.ds(start, L) if d == 1 else pl.ds(start, L, stride=d)

        def unit(u, carry, pi=pi, d=d, band=band, nblk=nblk, rows_at=rows_at):
            r = u % d
            n = u // d
            rows = rows_at(r + d * L * n)
            rows_prev_own = rows_at(r + d * L * jnp.maximum(n - 1, 0))
            rows_prev_span = rows_at(r + d * L * (nblk - 1))
            from_prev_span = n == 0
            valid = band & ((kj >= L) | jnp.logical_not(from_prev_span & first_span))
            q2 = q_ref[rows, :] * scale
            kp = jnp.where(from_prev_span, kp_ref[rows_prev_span, :], k_ref[rows_prev_own, :])
            vp = jnp.where(from_prev_span, vp_ref[rows_prev_span, :], v_ref[rows_prev_own, :])
            k2 = jnp.concatenate([kp, k_ref[rows, :]], axis=0).astype(BF16)
            v2 = jnp.concatenate([vp, v_ref[rows, :]], axis=0).astype(BF16)
            res = []
            for half in range(2):
                hmask = lo if half == 0 else jnp.logical_not(lo)
                qm = jnp.where(hmask, q2, 0.0).astype(BF16)
                s = lax.dot_general(qm, k2, _NT, preferred_element_type=F32)
                s = jnp.where(valid, s + bias_ref[pi, half], NEG_BIG)
                m = jnp.max(s, axis=-1, keepdims=True)
                p = jnp.exp(s - m)
                res.append((m, jnp.sum(p, axis=-1, keepdims=True),
                            jnp.dot(p.astype(BF16), v2, preferred_element_type=F32)))
            m_b = jnp.where(lo, res[0][0], res[1][0])
            l_b = jnp.where(lo, res[0][1], res[1][1])
            acc_b = jnp.where(lo, res[0][2], res[1][2])
            if pi > 0:
                m_old = m_sc[rows, :]
                m_new = jnp.maximum(m_old, m_b)
                w_old = jnp.exp(m_old - m_new)
                w_b = jnp.exp(m_b - m_new)
                l_b = w_old * l_sc[rows, :] + w_b * l_b
                acc_b = w_old * acc_sc[rows, :] + w_b * acc_b
                m_b = m_new
            if pi < last:
                m_sc[rows, :] = m_b
                l_sc[rows, :] = l_b
                acc_sc[rows, :] = acc_b
            else:
                o_ref[rows, :] = acc_b / l_b
            return carry

        lax.fori_loop(0, span // L, unit, 0, unroll=unroll)


def _dilated(proj, bias, patterns=A_PATTERNS, unroll=4):
    bsz, t, _ = proj.shape
    L = A_BLOCK
    span = L * max(d for _, d in patterns)
    assert t % span == 0
    npair = A_WIDTH // LANES
    qc, kc, vc = A_COL0 // LANES, A_COL0 // LANES + npair, A_COL0 // LANES + 2 * npair

    def own(c):
        return pl.BlockSpec((None, span, LANES), lambda b, h, n: (b, n, c + h))

    def prev(c):
        return pl.BlockSpec((None, span, LANES), lambda b, h, n: (b, jnp.maximum(n - 1, 0), c + h))

    return pl.pallas_call(
        functools.partial(_dilated_body, span=span, patterns=patterns, unroll=unroll),
        grid=(bsz, npair, t // span),
        in_specs=[own(qc), prev(kc), own(kc), prev(vc), own(vc),
                  pl.BlockSpec((len(patterns), 2, L, 2 * L), lambda b, h, n: (0, h, 0, 0))],
        out_specs=pl.BlockSpec((None, span, LANES), lambda b, h, n: (b, n, h)),
        out_shape=jax.ShapeDtypeStruct((bsz, t, A_WIDTH), F32),
        scratch_shapes=[pltpu.VMEM((span, LANES), F32)] * 3,
        compiler_params=pltpu.CompilerParams(
            dimension_semantics=("arbitrary", "arbitrary", "arbitrary"), vmem_limit_bytes=VMEM_LIMIT),
        name="dilated",
    )(proj, proj, proj, proj, proj, bias)


def _cumsum_rows(tri, x):
    hi = x.astype(BF16)
    r1 = x - hi.astype(F32)
    mid = r1.astype(BF16)
    lo = (r1 - mid.astype(F32)).astype(BF16)
    return sum(jnp.dot(tri, part, preferred_element_type=F32) for part in (hi, mid, lo))


def _hgrn_gates(z, lb):
    sig = _sigmoid(z)
    fg = lb + (1.0 - lb) * sig
    return jnp.log(jnp.maximum(fg, 1e-30)) * LOG2E, (1.0 - lb) * (1.0 - sig)


def _hgrn_intra_exact(q, k, b, v, lag, sub):
    C, V = v.shape
    amat = jnp.where(lag == 0, jnp.sum(q * k, axis=-1, keepdims=True), 0.0)
    for d in range(1, sub):
        ks = pltpu.roll(k, d, 0)
        bs = pltpu.roll(b, d, 0)
        pr = q * ks * jnp.exp2(jnp.minimum(b - bs, 0.0))
        amat = jnp.where(lag == d, jnp.sum(pr, axis=-1, keepdims=True), amat)
    o = jnp.dot(amat.astype(BF16), v.astype(BF16), preferred_element_type=F32)
    for j in range(C // sub - 1):
        e = sub * (j + 1)
        r = b[e - 1:e]
        kh = (k[e - sub:e] * jnp.exp2(r - b[e - sub:e])).astype(BF16)
        qt = (q[e:] * jnp.exp2(b[e:] - r)).astype(BF16)
        a = lax.dot_general(qt, kh, _NT, preferred_element_type=F32)
        contrib = jnp.dot(a.astype(BF16), v[e - sub:e].astype(BF16), preferred_element_type=F32)
        o = o + jnp.concatenate([jnp.zeros((e, V), F32), contrib], axis=0)
    return o


def _hgrn_intra_factored(q, k, b, v, causal, sub):
    C, K = q.shape
    qs, ks = [], []
    for j in range(C // sub):
        s0 = sub * j
        r = b[s0 - 1:s0] if j > 0 else jnp.zeros((1, K), F32)
        qt = (q[s0:] * jnp.exp2(b[s0:] - r)).astype(BF16)
        kh = (k[s0:s0 + sub] * jnp.exp2(r - b[s0:s0 + sub])).astype(BF16)
        above = [jnp.zeros((s0, K), BF16)] if s0 > 0 else []
        below = [jnp.zeros((C - s0 - sub, K), BF16)] if C - s0 - sub > 0 else []
        qs.append(jnp.concatenate(above + [qt], axis=0))
        ks.append(jnp.concatenate(above + [kh] + below, axis=0))
    a = lax.dot_general(jnp.concatenate(qs, axis=1), jnp.concatenate(ks, axis=1), _NT,
                        preferred_element_type=F32)
    a = jnp.where(causal, a, 0.0)
    return jnp.dot(a.astype(BF16), v.astype(BF16), preferred_element_type=F32)


def _hgrn_body(q_ref, f_ref, i_ref, g_ref, lb_ref, gn_ref, o_ref, st_ref, *, chunk, sub, fsub, nchunk):
    C = chunk
    K, V = B_KEY_DIM, B_VAL_DIM

    @pl.when(pl.program_id(1) == 0)
    def _():
        st_ref[...] = jnp.zeros_like(st_ref)

    ri = lax.broadcasted_iota(jnp.int32, (C, C), 0)
    ci = lax.broadcasted_iota(jnp.int32, (C, C), 1)
    causal = ri >= ci
    tri = causal.astype(F32).astype(BF16)
    lag = jnp.where((ri // sub == ci // sub) & causal, ri - ci, -1)

    def run(factored):
        def chunk_body(ic, carry):
            r0 = pl.multiple_of(ic * C, C)
            rows = pl.ds(r0, C)
            outs = []
            for h in range(B_HEADS):
                kc = slice(h * K, (h + 1) * K)
                vc = slice(h * V, (h + 1) * V)
                qraw = q_ref[0, rows, kc]
                v = i_ref[0, rows, vc]
                gate = g_ref[0, rows, vc]
                g2, k = _hgrn_gates(f_ref[0, rows, kc], lb_ref[:, kc])
                q = qraw * _sigmoid(qraw)
                b = _cumsum_rows(tri, g2)
                st = st_ref[h]
                o = lax.dot_general((q * jnp.exp2(b)).astype(BF16), st.astype(BF16), _NT,
                                    preferred_element_type=F32)
                if factored:
                    o = o + _hgrn_intra_factored(q, k, b, v, causal, fsub)
                else:
                    o = o + _hgrn_intra_exact(q, k, b, v, lag, sub)
                bl = b[C - 1:C]
                kd = (k * jnp.exp2(bl - b)).astype(BF16)
                st_ref[h] = st * jnp.exp2(bl) + jnp.dot(v.T.astype(BF16), kd, preferred_element_type=F32)
                on = _rms(o, gn_ref[...])
                outs.append(on * (gate * _sigmoid(gate)))
            o_ref[0, rows, :] = jnp.concatenate(outs, axis=-1)
            return carry

        lax.fori_loop(0, nchunk, chunk_body, 0, unroll=4 if factored else 1)

    g2_all, _ = _hgrn_gates(f_ref[0], lb_ref[...])
    tb = g2_all.shape[0]
    worst = jnp.max(-jnp.sum(g2_all.reshape(tb // fsub, fsub, g2_all.shape[1]), axis=1))
    lax.cond(worst < HGRN_FACTOR_LIMIT, lambda: run(True), lambda: run(False))


def _hgrn(proj, lb, gnorm, tb=256):
    bsz, t, _ = proj.shape
    C = HGRN_CHUNK
    assert t % tb == 0 and tb % C == 0
    kw, vw = B_QK_WIDTH, B_V_WIDTH
    return pl.pallas_call(
        functools.partial(_hgrn_body, chunk=C, sub=HGRN_SUB, fsub=HGRN_FSUB, nchunk=tb // C),
        grid=(bsz, t // tb),
        in_specs=[
            pl.BlockSpec((1, tb, kw), lambda b, i: (b, i, 0)),
            pl.BlockSpec((1, tb, kw), lambda b, i: (b, i, 1)),
            pl.BlockSpec((1, tb, vw), lambda b, i: (b, i, 2 * kw // vw)),
            pl.BlockSpec((1, tb, vw), lambda b, i: (b, i, 2 * kw // vw + 1)),
            pl.BlockSpec((1, kw), lambda b, i: (0, 0)),
            pl.BlockSpec((1, B_VAL_DIM), lambda b, i: (0, 0)),
        ],
        out_specs=pl.BlockSpec((1, tb, vw), lambda b, i: (b, i, 0)),
        out_shape=jax.ShapeDtypeStruct((bsz, t, vw), F32),
        scratch_shapes=[pltpu.VMEM((B_HEADS, B_VAL_DIM, B_KEY_DIM), F32)],
        compiler_params=pltpu.CompilerParams(
            dimension_semantics=("arbitrary", "arbitrary"), vmem_limit_bytes=VMEM_LIMIT),
        name="hgrn2",
    )(proj, proj, proj, proj, lb, gnorm)


def _moba_body(q_ref, k_ref, v_ref, bias_ref, o_ref, kbh, vtb, kmean, qx, sbuf, pbuf, acc,
               *, nb, ntile, topk, unroll):
    L = C_BLOCK
    E = HEAD_DIM
    G = LANES // 4
    U = unroll
    qb = pl.program_id(2)
    lane = lax.broadcasted_iota(jnp.int32, (L, LANES), 1)

    @pl.when(qb == 0)
    def _():
        kmean[...] = jnp.zeros_like(kmean)
        ones_rows = jnp.ones((VT_ROWS - E, L), BF16)
        for n in range(nb):
            blk = k_ref[0, n * L:(n + 1) * L, :]
            mean = jnp.mean(blk, axis=0, keepdims=True)
            for g in range(LANES // G):
                kmean[g * G + n:g * G + n + 1, :] = mean
            kbh[n, 0] = jnp.where(lane < E, blk, (lane == 2 * G + n).astype(F32)).astype(BF16)
            kbh[n, 1] = jnp.where(lane >= E, blk, (lane == n).astype(F32)).astype(BF16)
            vt = v_ref[0, n * L:(n + 1) * L, :].T.astype(BF16)
            for half in range(2):
                vtb[n, half, 0:E, :] = vt[half * E:(half + 1) * E, :]
                vtb[n, half, E:, :] = ones_rows
        kbh[nb, 0] = (lane == 3 * G).astype(BF16)
        kbh[nb, 1] = (lane == G).astype(BF16)

    q2 = q_ref[0]
    slot = lane % G
    grp = lane // G
    slot_f = slot.astype(F32)
    qscale = (E ** -0.5) * LOG2E

    for half in range(2):
        hmask = (lane < E) if half == 0 else (lane >= E)
        qf = jnp.where(hmask, q2, 0.0)
        gate = lax.dot_general(qf, kmean[...], _NT, preferred_element_type=F32,
                               precision=lax.Precision.HIGHEST)
        gate = jnp.where(slot < qb, gate, NEG_BIG)
        picked = jnp.zeros((L, LANES), jnp.bool_)
        for _ in range(topk):
            mx = jnp.max(gate, axis=-1, keepdims=True)
            ix = jnp.min(jnp.where(gate == mx, slot_f, float(G)), axis=-1, keepdims=True)
            hit = slot_f == ix
            picked = picked | (hit & (mx > 0.5 * NEG_BIG))
            gate = jnp.where(hit, -jnp.inf, gate)
        wanted = picked | (slot == qb)
        sel_grp = 2 if half == 0 else 0
        ext = jnp.where((grp == sel_grp + 1) | ((grp == sel_grp) & jnp.logical_not(wanted)), NEG_BIG, 0.0)
        qx[half] = jnp.where(hmask, qf * qscale, ext).astype(BF16)

    def scores_stage(dist, slot_):
        kb = jnp.where(dist <= qb, qb - dist, nb)
        tile = jnp.minimum(dist, ntile - 1)
        for half in range(2):
            s = lax.dot_general(kbh[kb, half], qx[half], _NT, preferred_element_type=F32)
            sbuf[slot_, half] = s + bias_ref[half, tile]

    def softmax_stage(slot_, m_old):
        out = []
        for half in range(2):
            s = sbuf[slot_, half]
            m_new = jnp.maximum(m_old[half], jnp.max(s, axis=0, keepdims=True))
            pbuf[slot_, half] = jnp.exp2(s - m_new).astype(BF16)
            out.append(m_new)
        return tuple(out)

    def pv_stage(dist, slot_, alpha):
        kb = jnp.clip(qb - dist, 0, nb - 1)
        for half in range(2):
            acc[half] = alpha[half] * acc[half] + jnp.dot(vtb[kb, half], pbuf[slot_, half],
                                                          preferred_element_type=F32)

    pbuf[...] = jnp.zeros_like(pbuf)
    acc[...] = jnp.zeros_like(acc)
    scores_stage(0, 0)

    def trip(j, carry):
        m_old, alpha = carry[0:2], carry[2:4]
        for u in range(U):
            dist = j * U + u
            pv_stage(dist - 1, (u - 1) % U, alpha)
            m_new = softmax_stage(u, m_old)
            scores_stage(dist + 1, (u + 1) % U)
            alpha = tuple(jnp.exp2(a - b) for a, b in zip(m_old, m_new))
            m_old = m_new
        return m_old + alpha

    row = jnp.full((1, L), NEG_BIG, F32)
    one = jnp.ones((1, L), F32)
    ntrips = (qb + U) // U
    carry = lax.fori_loop(0, ntrips, trip, (row, row, one, one))
    pv_stage(ntrips * U - 1, (U - 1) % U, carry[2:4])
    out_t = jnp.concatenate([acc[half, 0:E, :] / acc[half, E:E + 1, :] for half in range(2)], axis=0)
    o_ref[0] = out_t.T


def _moba(proj, bias_tiles, unroll=2):
    bsz, t, _ = proj.shape
    L = C_BLOCK
    assert t % L == 0
    nb = t // L
    assert nb <= LANES // 4, "one selection lane per key block and lane group"
    ntile = bias_tiles.shape[1]
    qcol, kcol, vcol = (C_COL0 // LANES, (C_COL0 + C_WIDTH) // LANES, (C_COL0 + 2 * C_WIDTH) // LANES)
    return pl.pallas_call(
        functools.partial(_moba_body, nb=nb, ntile=ntile, topk=min(C_TOPK, nb), unroll=unroll),
        grid=(bsz, C_HEADS // 2, nb),
        in_specs=[
            pl.BlockSpec((1, L, LANES), lambda b, h, i: (b, i, qcol + h)),
            pl.BlockSpec((1, t, LANES), lambda b, h, i: (b, 0, kcol + h)),
            pl.BlockSpec((1, t, LANES), lambda b, h, i: (b, 0, vcol + h)),
            pl.BlockSpec((2, ntile, L, L), lambda b, h, i: (h, 0, 0, 0)),
        ],
        out_specs=pl.BlockSpec((1, L, LANES), lambda b, h, i: (b, i, h)),
        out_shape=jax.ShapeDtypeStruct((bsz, t, C_WIDTH), F32),
        scratch_shapes=[
            pltpu.VMEM((nb + 1, 2, L, LANES), BF16),
            pltpu.VMEM((nb, 2, VT_ROWS, L), BF16),
            pltpu.VMEM((LANES, LANES), F32),
            pltpu.VMEM((2, L, LANES), BF16),
            pltpu.VMEM((unroll, 2, L, L), F32),
            pltpu.VMEM((unroll, 2, L, L), BF16),
            pltpu.VMEM((2, VT_ROWS, L), F32),
        ],
        compiler_params=pltpu.CompilerParams(
            dimension_semantics=("arbitrary", "arbitrary", "arbitrary"),
            vmem_limit_bytes=VMEM_LIMIT),
        name="moba",
    )(proj, proj, proj, bias_tiles)


def _post_body(x_ref, ya_ref, yb_ref, yc_ref, p_ref,
               wout_ref, nffn_ref, wup_ref, cw_ref, cb_ref, wdown_ref, nple_ref, wpg_ref, wpe_ref,
               nfin_ref, out_ref, carry_ref, *, final, fch):
    tm = x_ref.shape[1]

    @pl.when(pl.program_id(1) == 0)
    def _():
        carry_ref[...] = jnp.zeros_like(carry_ref)

    x = x_ref[0]
    x = x + jnp.dot(ya_ref[0].astype(BF16), wout_ref[0:A_WIDTH, :], preferred_element_type=F32)
    x = x + jnp.dot(yb_ref[0].astype(BF16), wout_ref[A_WIDTH:A_WIDTH + B_V_WIDTH, :],
                    preferred_element_type=F32)
    x = x + jnp.dot(yc_ref[0].astype(BF16), wout_ref[A_WIDTH + B_V_WIDTH:, :],
                    preferred_element_type=F32)

    h = _rms(x, nffn_ref[...]).astype(BF16)
    row = lax.broadcasted_iota(jnp.int32, (tm, 1), 0)

    def conv_cols(c0):
        cols = slice(c0, c0 + fch)
        u = jnp.dot(h, wup_ref[:, cols], preferred_element_type=F32)
        prev = carry_ref[:, cols]
        carry_ref[:, cols] = u[tm - 8:tm]
        u1 = jnp.where(row == 0, prev[7:8], pltpu.roll(u, 1, 0))
        u2 = jnp.where(row == 0, prev[6:7], jnp.where(row == 1, prev[7:8], pltpu.roll(u, 2, 0)))
        w = cw_ref[:, cols]
        return w[0:1] * u2 + w[1:2] * u1 + w[2:3] * u + cb_ref[:, cols]

    acc = jnp.zeros(x.shape, F32)
    for j in range(D_FF // fch):
        ug = conv_cols(j * fch)
        uv = conv_cols(D_FF + j * fch)
        act = (ug * _sigmoid(ug)) * uv
        acc = acc + jnp.dot(act.astype(BF16), wdown_ref[j * fch:(j + 1) * fch, :],
                            preferred_element_type=F32)
    x = x + acc

    hp = _rms(x, nple_ref[...]).astype(BF16)
    gate = _sigmoid(jnp.dot(hp, wpg_ref[...], preferred_element_type=F32))
    x = x + gate * jnp.dot(p_ref[0].astype(BF16), wpe_ref[...], preferred_element_type=F32)
    if final:
        x = _rms(x, nfin_ref[...])
    out_ref[0] = x


def _post(x, ya, yb, yc, p, wout, nffn, wup, cw, cb, wdown, nple, wpg, wpe, nfin, final,
          tm=512, fch=256):
    bsz, t, d = x.shape
    assert t % tm == 0 and D_FF % fch == 0

    def tok(width):
        return pl.BlockSpec((1, tm, width), lambda b, i: (b, i, 0))

    def const(shape):
        return pl.BlockSpec(shape, lambda b, i: (0,) * len(shape), pipeline_mode=pl.Buffered(1))

    return pl.pallas_call(
        functools.partial(_post_body, final=final, fch=fch),
        grid=(bsz, t // tm),
        in_specs=[tok(d), tok(A_WIDTH), tok(B_V_WIDTH), tok(C_WIDTH), tok(PLE_DIM),
                  const(wout.shape), const(nffn.shape), const(wup.shape), const(cw.shape),
                  const(cb.shape), const(wdown.shape), const(nple.shape), const(wpg.shape),
                  const(wpe.shape), const(nfin.shape)],
        out_specs=tok(d),
        out_shape=jax.ShapeDtypeStruct((bsz, t, d), F32),
        scratch_shapes=[pltpu.VMEM((8, 2 * D_FF), F32)],
        compiler_params=pltpu.CompilerParams(
            dimension_semantics=("arbitrary", "arbitrary"), vmem_limit_bytes=VMEM_LIMIT),
        name="post",
    )(x, ya, yb, yc, p, wout, nffn, wup, cw, cb, wdown, nple, wpg, wpe, nfin)


def _dilated_bias(bias_a, dilation):
    L = A_BLOCK
    m = jnp.arange(3 * L - 1)
    u = bias_a[:, _rel_bucket((2 * L - 1 - m) * dilation)]
    return _toeplitz(u, L, 2 * L)


def _moba_bias(bias_c, t):
    L = C_BLOCK
    nb = t // L
    dist = np.arange(0, t)
    bk = _rel_bucket_np(dist)
    nfar = nb
    for dblk in range(nb, 0, -1):
        if np.all(bk[max(dblk * L - (L - 1), 0):] == bk[-1]):
            nfar = dblk
    m = jnp.arange(2 * L - 1)
    tiles = []
    for dblk in range(nfar):
        u = bias_c[:, _rel_bucket(dblk * L + m - (L - 1))]
        tiles.append(_toeplitz(u, L, L))
    far = jnp.broadcast_to(bias_c[:, int(bk[-1])][:, None, None], (bias_c.shape[0], L, L))
    tiles.append(far)
    kj = jnp.arange(L)[:, None]
    qi = jnp.arange(L)[None, :]
    tiles[0] = jnp.where(qi >= kj, tiles[0] * LOG2E, NEG_BIG)
    tiles[1:] = [tl * LOG2E for tl in tiles[1:]]
    return jnp.stack(tiles, axis=1)


def kernel(x, p, rel_bias, norm_mix, w_in, w_out, lower_bounds, hgrn_gnorm, norm_ffn, w_up, conv_w,
           conv_b, w_down, norm_ple, w_pe, w_pg, norm_final):
    bsz, t, d = x.shape
    depth = w_in.shape[0]
    lb_sm = jax.nn.softmax(lower_bounds.astype(F32), axis=0)
    lb_all = jnp.cumsum(lb_sm, axis=0) - lb_sm[0]
    bias_a = rel_bias[:A_HEADS].astype(F32)
    bias_c = rel_bias[A_HEADS:].astype(F32)
    bias_a_tiles = jnp.stack([_dilated_bias(bias_a, dil) for _, dil in A_PATTERNS])
    bias_c_tiles = _moba_bias(bias_c, t)

    a_end = 3 * A_WIDTH
    b_end = a_end + B_GROUP_WIDTH
    for i in range(depth):
        w = w_in[i]
        w_perm = jnp.concatenate([w[:, a_end:b_end], w[:, :a_end], w[:, b_end:]], axis=1).astype(BF16)
        proj = _inproj(x.reshape(bsz * t, d), norm_mix[i][None], w_perm).reshape(bsz, t, IN_WIDTH)
        ya = _dilated(proj, bias_a_tiles)
        yb = _hgrn(proj, lb_all[i][None], hgrn_gnorm[i].astype(F32)[None])
        yc = _moba(proj, bias_c_tiles)
        x = _post(x, ya, yb, yc, p[i], w_out[i].astype(BF16), norm_ffn[i][None],
                  w_up[i].astype(BF16), conv_w[i], conv_b[i][None], w_down[i].astype(BF16),
                  norm_ple[i][None], w_pg[i].astype(BF16), w_pe[i].astype(BF16), norm_final[None],
                  final=(i == depth - 1))
    return x
```

```python
import functools
import math

import numpy as np
import jax
import jax.numpy as jnp
from jax import lax
from jax.experimental import pallas as pl
from jax.experimental.pallas import tpu as pltpu

F32 = jnp.float32
BF16 = jnp.bfloat16

HEAD_DIM = 64
A_HEADS = 6
A_PATTERNS = ((128, 1), (512, 4), (2048, 16))
A_BLOCK = 128
B_HEADS = 4
B_KEY_DIM = 128
B_VAL_DIM = 64
C_HEADS = 6
C_BLOCK = 256
C_TOPK = 3
N_BUCKETS = 32
MAX_DISTANCE = 2048
D_FF = 2816
PLE_DIM = 256
EPS = 1e-6
NEG_BIG = -1e30

A_WIDTH = A_HEADS * HEAD_DIM
B_QK_WIDTH = B_HEADS * B_KEY_DIM
B_V_WIDTH = B_HEADS * B_VAL_DIM
C_WIDTH = C_HEADS * HEAD_DIM
IN_WIDTH = 3 * A_WIDTH + 2 * B_QK_WIDTH + 2 * B_V_WIDTH + 3 * C_WIDTH
LANES = 128

B_GROUP_WIDTH = 2 * B_QK_WIDTH + 2 * B_V_WIDTH
A_COL0 = B_GROUP_WIDTH
C_COL0 = A_COL0 + 3 * A_WIDTH

LOG2E = 1.4426950408889634
VT_ROWS = HEAD_DIM + 16

HGRN_CHUNK = 64
HGRN_SUB = 8
HGRN_FSUB = 16
HGRN_FACTOR_LIMIT = 64.0
VMEM_LIMIT = 56 * 1024 * 1024

_NT = (((1,), (1,)), ((), ()))


def _rel_bucket(dist):
    max_exact = N_BUCKETS // 2
    d = jnp.maximum(dist, 0)
    df = jnp.maximum(d, max_exact).astype(F32)
    large = max_exact + (jnp.log(df / max_exact) / math.log(MAX_DISTANCE / max_exact)
                         * (N_BUCKETS - max_exact)).astype(jnp.int32)
    large = jnp.minimum(large, N_BUCKETS - 1)
    return jnp.where(d < max_exact, d, large)


def _rel_bucket_np(dist):
    max_exact = N_BUCKETS // 2
    d = np.maximum(dist, 0)
    df = np.maximum(d, max_exact).astype(np.float64)
    large = max_exact + (np.log(df / max_exact) / math.log(MAX_DISTANCE / max_exact)
                         * (N_BUCKETS - max_exact)).astype(np.int64)
    large = np.minimum(large, N_BUCKETS - 1)
    return np.where(d < max_exact, d, large)


def _toeplitz(u, rows, cols):
    n = rows + cols - 1
    w = jnp.concatenate([u, jnp.zeros(u.shape[:-1] + (1,), u.dtype)], axis=-1)
    flat = jnp.tile(w, rows)[..., : rows * n]
    m = flat.reshape(u.shape[:-1] + (rows, n))
    return m[..., rows - 1: rows - 1 + cols]


def _rms(x, g):
    return x * lax.rsqrt(jnp.mean(x * x, axis=-1, keepdims=True) + EPS) * g


def _sigmoid(x):
    return 1.0 / (1.0 + jnp.exp(-x))


def _inproj_body(x_ref, g_ref, w_ref, o_ref, *, nch):
    h = _rms(x_ref[...], g_ref[...]).astype(BF16)
    for j in range(0, w_ref.shape[1], nch):
        o_ref[:, j:j + nch] = jnp.dot(h, w_ref[:, j:j + nch], preferred_element_type=F32)


def _inproj(x2, gain, w_bf, tm=512, nch=768):
    n, d = x2.shape
    width = w_bf.shape[1]
    return pl.pallas_call(
        functools.partial(_inproj_body, nch=nch),
        grid=(n // tm,),
        in_specs=[
            pl.BlockSpec((tm, d), lambda i: (i, 0)),
            pl.BlockSpec((1, d), lambda i: (0, 0)),
            pl.BlockSpec((d, width), lambda i: (0, 0), pipeline_mode=pl.Buffered(1)),
        ],
        out_specs=pl.BlockSpec((tm, width), lambda i: (i, 0)),
        out_shape=jax.ShapeDtypeStruct((n, width), F32),
        compiler_params=pltpu.CompilerParams(
            dimension_semantics=("arbitrary",), vmem_limit_bytes=VMEM_LIMIT),
        name="inproj",
    )(x2, gain, w_bf)


def _dilated_body(q_ref, kp_ref, k_ref, vp_ref, v_ref, bias_ref, o_ref, m_sc, l_sc, acc_sc,
                  *, span, patterns, unroll):
    L = A_BLOCK
    first_span = pl.program_id(2) == 0
    qi = lax.broadcasted_iota(jnp.int32, (L, 2 * L), 0)
    kj = lax.broadcasted_iota(jnp.int32, (L, 2 * L), 1)
    delta = qi + L - kj
    lo = lax.broadcasted_iota(jnp.int32, (L, LANES), 1) < HEAD_DIM
    scale = HEAD_DIM ** -0.5
    last = len(patterns) - 1

    for pi, (window, d) in enumerate(patterns):
        band = (delta >= 0) & (delta <= window // d)
        nblk = span // (L * d)

        def rows_at(start, d=d):
            return pl.ds(start, L) if d == 1 else pl.ds(start, L, stride=d)

        def unit(u, carry, pi=pi, d=d, band=band, nblk=nblk, rows_at=rows_at):
            r = u % d
            n = u // d
            rows = rows_at(r + d * L * n)
            rows_prev_own = rows_at(r + d * L * jnp.maximum(n - 1, 0))
            rows_prev_span = rows_at(r + d * L * (nblk - 1))
            from_prev_span = n == 0
            valid = band & ((kj >= L) | jnp.logical_not(from_prev_span & first_span))
            q2 = q_ref[rows, :] * scale
            kp = jnp.where(from_prev_span, kp_ref[rows_prev_span, :], k_ref[rows_prev_own, :])
            vp = jnp.where(from_prev_span, vp_ref[rows_prev_span, :], v_ref[rows_prev_own, :])
            k2 = jnp.concatenate([kp, k_ref[rows, :]], axis=0).astype(BF16)
            v2 = jnp.concatenate([vp, v_ref[rows, :]], axis=0).astype(BF16)
            res = []
            for half in range(2):
                hmask = lo if half == 0 else jnp.logical_not(lo)
                qm = jnp.where(hmask, q2, 0.0).astype(BF16)
                s = lax.dot_general(qm, k2, _NT, preferred_element_type=F32)
                s = jnp.where(valid, s + bias_ref[pi, half], NEG_BIG)
                m = jnp.max(s, axis=-1, keepdims=True)
                p = jnp.exp(s - m)
                res.append((m, jnp.sum(p, axis=-1, keepdims=True),
                            jnp.dot(p.astype(BF16), v2, preferred_element_type=F32)))
            m_b = jnp.where(lo, res[0][0], res[1][0])
            l_b = jnp.where(lo, res[0][1], res[1][1])
            acc_b = jnp.where(lo, res[0][2], res[1][2])
            if pi > 0:
                m_old = m_sc[rows, :]
                m_new = jnp.maximum(m_old, m_b)
                w_old = jnp.exp(m_old - m_new)
                w_b = jnp.exp(m_b - m_new)
                l_b = w_old * l_sc[rows, :] + w_b * l_b
                acc_b = w_old * acc_sc[rows, :] + w_b * acc_b
                m_b = m_new
            if pi < last:
                m_sc[rows, :] = m_b
                l_sc[rows, :] = l_b
                acc_sc[rows, :] = acc_b
            else:
                o_ref[rows, :] = acc_b / l_b
            return carry

        lax.fori_loop(0, span // L, unit, 0, unroll=unroll)


def _dilated(proj, bias, patterns=A_PATTERNS, unroll=8):
    bsz, t, _ = proj.shape
    L = A_BLOCK
    span = L * max(d for _, d in patterns)
    assert t % span == 0
    npair = A_WIDTH // LANES
    qc, kc, vc = A_COL0 // LANES, A_COL0 // LANES + npair, A_COL0 // LANES + 2 * npair

    def own(c):
        return pl.BlockSpec((None, span, LANES), lambda b, h, n: (b, n, c + h))

    def prev(c):
        return pl.BlockSpec((None, span, LANES), lambda b, h, n: (b, jnp.maximum(n - 1, 0), c + h))

    return pl.pallas_call(
        functools.partial(_dilated_body, span=span, patterns=patterns, unroll=unroll),
        grid=(bsz, npair, t // span),
        in_specs=[own(qc), prev(kc), own(kc), prev(vc), own(vc),
                  pl.BlockSpec((len(patterns), 2, L, 2 * L), lambda b, h, n: (0, h, 0, 0))],
        out_specs=pl.BlockSpec((None, span, LANES), lambda b, h, n: (b, n, h)),
        out_shape=jax.ShapeDtypeStruct((bsz, t, A_WIDTH), F32),
        scratch_shapes=[pltpu.VMEM((span, LANES), F32)] * 3,
        compiler_params=pltpu.CompilerParams(
            dimension_semantics=("arbitrary", "arbitrary", "arbitrary"), vmem_limit_bytes=VMEM_LIMIT),
        name="dilated",
    )(proj, proj, proj, proj, proj, bias)


def _cumsum_rows(tri, x):
    hi = x.astype(BF16)
    r1 = x - hi.astype(F32)
    mid = r1.astype(BF16)
    lo = (r1 - mid.astype(F32)).astype(BF16)
    return sum(jnp.dot(tri, part, preferred_element_type=F32) for part in (hi, mid, lo))


def _hgrn_gates(z, lb):
    sig = _sigmoid(z)
    fg = lb + (1.0 - lb) * sig
    return jnp.log(jnp.maximum(fg, 1e-30)) * LOG2E, (1.0 - lb) * (1.0 - sig)


def _hgrn_intra_exact(q, k, b, v, lag, sub):
    C, V = v.shape
    amat = jnp.where(lag == 0, jnp.sum(q * k, axis=-1, keepdims=True), 0.0)
    for d in range(1, sub):
        ks = pltpu.roll(k, d, 0)
        bs = pltpu.roll(b, d, 0)
        pr = q * ks * jnp.exp2(jnp.minimum(b - bs, 0.0))
        amat = jnp.where(lag == d, jnp.sum(pr, axis=-1, keepdims=True), amat)
    o = jnp.dot(amat.astype(BF16), v.astype(BF16), preferred_element_type=F32)
    for j in range(C // sub - 1):
        e = sub * (j + 1)
        r = b[e - 1:e]
        kh = (k[e - sub:e] * jnp.exp2(r - b[e - sub:e])).astype(BF16)
        qt = (q[e:] * jnp.exp2(b[e:] - r)).astype(BF16)
        a = lax.dot_general(qt, kh, _NT, preferred_element_type=F32)
        contrib = jnp.dot(a.astype(BF16), v[e - sub:e].astype(BF16), preferred_element_type=F32)
        o = o + jnp.concatenate([jnp.zeros((e, V), F32), contrib], axis=0)
    return o


def _hgrn_intra_factored(q, k, b, v, causal, sub):
    C, K = q.shape
    qs, ks = [], []
    for j in range(C // sub):
        s0 = sub * j
        r = b[s0 - 1:s0] if j > 0 else jnp.zeros((1, K), F32)
        qt = (q[s0:] * jnp.exp2(b[s0:] - r)).astype(BF16)
        kh = (k[s0:s0 + sub] * jnp.exp2(r - b[s0:s0 + sub])).astype(BF16)
        above = [jnp.zeros((s0, K), BF16)] if s0 > 0 else []
        below = [jnp.zeros((C - s0 - sub, K), BF16)] if C - s0 - sub > 0 else []
        qs.append(jnp.concatenate(above + [qt], axis=0))
        ks.append(jnp.concatenate(above + [kh] + below, axis=0))
    a = lax.dot_general(jnp.concatenate(qs, axis=1), jnp.concatenate(ks, axis=1), _NT,
                        preferred_element_type=F32)
    a = jnp.where(causal, a, 0.0)
    return jnp.dot(a.astype(BF16), v.astype(BF16), preferred_element_type=F32)


def _hgrn_body(q_ref, f_ref, i_ref, g_ref, lb_ref, gn_ref, o_ref, st_ref, *, chunk, sub, fsub, nchunk):
    C = chunk
    K, V = B_KEY_DIM, B_VAL_DIM

    @pl.when(pl.program_id(1) == 0)
    def _():
        st_ref[...] = jnp.zeros_like(st_ref)

    ri = lax.broadcasted_iota(jnp.int32, (C, C), 0)
    ci = lax.broadcasted_iota(jnp.int32, (C, C), 1)
    causal = ri >= ci
    tri = causal.astype(F32).astype(BF16)
    lag = jnp.where((ri // sub == ci // sub) & causal, ri - ci, -1)

    def run(factored):
        def chunk_body(ic, carry):
            r0 = pl.multiple_of(ic * C, C)
            rows = pl.ds(r0, C)
            outs = []
            for h in range(B_HEADS):
                kc = slice(h * K, (h + 1) * K)
                vc = slice(h * V, (h + 1) * V)
                qraw = q_ref[0, rows, kc]
                v = i_ref[0, rows, vc]
                gate = g_ref[0, rows, vc]
                g2, k = _hgrn_gates(f_ref[0, rows, kc], lb_ref[:, kc])
                q = qraw * _sigmoid(qraw)
                b = _cumsum_rows(tri, g2)
                st = st_ref[h]
                o = lax.dot_general((q * jnp.exp2(b)).astype(BF16), st.astype(BF16), _NT,
                                    preferred_element_type=F32)
                if factored:
                    o = o + _hgrn_intra_factored(q, k, b, v, causal, fsub)
                else:
                    o = o + _hgrn_intra_exact(q, k, b, v, lag, sub)
                bl = b[C - 1:C]
                kd = (k * jnp.exp2(bl - b)).astype(BF16)
                st_ref[h] = st * jnp.exp2(bl) + jnp.dot(v.T.astype(BF16), kd, preferred_element_type=F32)
                on = _rms(o, gn_ref[...])
                outs.append(on * (gate * _sigmoid(gate)))
            o_ref[0, rows, :] = jnp.concatenate(outs, axis=-1)
            return carry

        lax.fori_loop(0, nchunk, chunk_body, 0, unroll=4 if factored else 1)

    g2_all, _ = _hgrn_gates(f_ref[0], lb_ref[...])
    tb = g2_all.shape[0]
    worst = jnp.max(-jnp.sum(g2_all.reshape(tb // fsub, fsub, g2_all.shape[1]), axis=1))
    lax.cond(worst < HGRN_FACTOR_LIMIT, lambda: run(True), lambda: run(False))


def _hgrn(proj, lb, gnorm, tb=256):
    bsz, t, _ = proj.shape
    C = HGRN_CHUNK
    assert t % tb == 0 and tb % C == 0
    kw, vw = B_QK_WIDTH, B_V_WIDTH
    return pl.pallas_call(
        functools.partial(_hgrn_body, chunk=C, sub=HGRN_SUB, fsub=HGRN_FSUB, nchunk=tb // C),
        grid=(bsz, t // tb),
        in_specs=[
            pl.BlockSpec((1, tb, kw), lambda b, i: (b, i, 0)),
            pl.BlockSpec((1, tb, kw), lambda b, i: (b, i, 1)),
            pl.BlockSpec((1, tb, vw), lambda b, i: (b, i, 2 * kw // vw)),
            pl.BlockSpec((1, tb, vw), lambda b, i: (b, i, 2 * kw // vw + 1)),
            pl.BlockSpec((1, kw), lambda b, i: (0, 0)),
            pl.BlockSpec((1, B_VAL_DIM), lambda b, i: (0, 0)),
        ],
        out_specs=pl.BlockSpec((1, tb, vw), lambda b, i: (b, i, 0)),
        out_shape=jax.ShapeDtypeStruct((bsz, t, vw), F32),
        scratch_shapes=[pltpu.VMEM((B_HEADS, B_VAL_DIM, B_KEY_DIM), F32)],
        compiler_params=pltpu.CompilerParams(
            dimension_semantics=("arbitrary", "arbitrary"), vmem_limit_bytes=VMEM_LIMIT),
        name="hgrn2",
    )(proj, proj, proj, proj, lb, gnorm)


def _moba_select_body(q_ref, k_ref, qx_ref, kmean, *, nb, topk, qchunk):
    L = C_BLOCK
    E = HEAD_DIM
    G = LANES // 4
    kmean[...] = jnp.zeros_like(kmean)
    for n in range(nb):
        kmean[n:n + 1, :] = jnp.mean(k_ref[0, n * L:(n + 1) * L, :], axis=0, keepdims=True)
    lane = lax.broadcasted_iota(jnp.int32, (qchunk, LANES), 1)
    grp = lane // G
    blk_i = lax.broadcasted_iota(jnp.int32, (G, qchunk), 0)
    blk_f = blk_i.astype(F32)
    qcol = lax.broadcasted_iota(jnp.int32, (G, qchunk), 1)
    place_r = lax.broadcasted_iota(jnp.int32, (G, LANES), 0)
    place_c = lax.broadcasted_iota(jnp.int32, (G, LANES), 1)
    qscale = (E ** -0.5) * LOG2E

    def chunk_body(c, carry):
        r0 = pl.multiple_of(c * qchunk, qchunk)
        q2 = q_ref[0, pl.ds(r0, qchunk), :]
        qblk = (qcol + r0) // L
        for half in range(2):
            hmask = (lane < E) if half == 0 else (lane >= E)
            qf = jnp.where(hmask, q2, 0.0)
            gate = lax.dot_general(kmean[...], qf, _NT, preferred_element_type=F32,
                                   precision=lax.Precision.HIGHEST)
            gate = jnp.where(blk_i < qblk, gate, NEG_BIG)
            picked = jnp.zeros((G, qchunk), jnp.bool_)
            for _ in range(topk):
                mx = jnp.max(gate, axis=0, keepdims=True)
                ix = jnp.min(jnp.where(gate == mx, blk_f, float(G)), axis=0, keepdims=True)
                hit = blk_f == ix
                picked = picked | (hit & (mx > 0.5 * NEG_BIG))
                gate = jnp.where(hit, -jnp.inf, gate)
            wanted = picked | (blk_i == qblk)
            sel_grp = 2 if half == 0 else 0
            place = (place_c == place_r + sel_grp * G).astype(BF16)
            flags = jnp.dot(wanted.astype(F32).T.astype(BF16), place, preferred_element_type=F32)
            ext = jnp.where((grp == sel_grp + 1) | ((grp == sel_grp) & (flags < 0.5)), NEG_BIG, 0.0)
            qx_ref[0, half, pl.ds(r0, qchunk), :] = jnp.where(hmask, qf * qscale, ext).astype(BF16)
        return carry

    lax.fori_loop(0, q_ref.shape[1] // qchunk, chunk_body, 0)


def _moba_body(qx, k_ref, v_ref, bias_ref, o_ref, kbh, vtb, sbuf, pbuf, acc,
               *, nb, ntile, unroll):
    L = C_BLOCK
    E = HEAD_DIM
    G = LANES // 4
    U = unroll
    LEAD = 1
    assert U >= 2 * LEAD
    qb = pl.program_id(2)
    lane = lax.broadcasted_iota(jnp.int32, (L, LANES), 1)

    @pl.when(qb == 0)
    def _():
        ones_rows = jnp.ones((VT_ROWS - E, L), BF16)
        for n in range(nb):
            blk = k_ref[0, n * L:(n + 1) * L, :]
            kbh[n, 0] = jnp.where(lane < E, blk, (lane == 2 * G + n).astype(F32)).astype(BF16)
            kbh[n, 1] = jnp.where(lane >= E, blk, (lane == n).astype(F32)).astype(BF16)
            vt = v_ref[0, n * L:(n + 1) * L, :].T.astype(BF16)
            for half in range(2):
                vtb[n, half, 0:E, :] = vt[half * E:(half + 1) * E, :]
                vtb[n, half, E:, :] = ones_rows
        kbh[nb, 0] = (lane == 3 * G).astype(BF16)
        kbh[nb, 1] = (lane == G).astype(BF16)

    def scores_stage(dist, slot_):
        kb = jnp.where(dist <= qb, qb - dist, nb)
        tile = jnp.minimum(dist, ntile - 1)
        for half in range(2):
            s = lax.dot_general(kbh[kb, half], qx[0, half], _NT, preferred_element_type=F32)
            sbuf[slot_, half] = s + bias_ref[half, tile]

    def softmax_stage(slot_, m_old):
        out = []
        for half in range(2):
            s = sbuf[slot_, half]
            m_new = jnp.maximum(m_old[half], jnp.max(s, axis=0, keepdims=True))
            pbuf[slot_, half] = jnp.exp2(s - m_new).astype(BF16)
            out.append(m_new)
        return tuple(out)

    def pv_stage(dist, slot_, alpha):
        kb = jnp.clip(qb - dist, 0, nb - 1)
        for half in range(2):
            acc[half] = alpha[half] * acc[half] + jnp.dot(vtb[kb, half], pbuf[slot_, half],
                                                          preferred_element_type=F32)

    pbuf[...] = jnp.zeros_like(pbuf)
    acc[...] = jnp.zeros_like(acc)
    for d0 in range(LEAD):
        scores_stage(d0, d0 % U)

    def trip(j, carry):
        m_old, alphas = carry[0:2], list(carry[2:])
        for u in range(U):
            dist = j * U + u
            pv_stage(dist - LEAD, (u - LEAD) % U, alphas[0:2])
            m_new = softmax_stage(u, m_old)
            scores_stage(dist + LEAD, (u + LEAD) % U)
            alphas = alphas[2:] + [jnp.exp2(a - b) for a, b in zip(m_old, m_new)]
            m_old = m_new
        return tuple(m_old) + tuple(alphas)

    row = jnp.full((1, L), NEG_BIG, F32)
    one = jnp.ones((1, L), F32)
    ntrips = (qb + U) // U
    carry = lax.fori_loop(0, ntrips, trip, (row, row) + (one,) * (2 * LEAD))
    for t in range(LEAD):
        pv_stage(ntrips * U - LEAD + t, (t - LEAD) % U, carry[2 + 2 * t:4 + 2 * t])
    out_t = jnp.concatenate([acc[half, 0:E, :] / acc[half, E:E + 1, :] for half in range(2)], axis=0)
    o_ref[0] = out_t.T


def _moba(proj, bias_tiles, unroll=2):
    bsz, t, _ = proj.shape
    L = C_BLOCK
    assert t % L == 0
    nb = t // L
    assert nb <= LANES // 4, "one selection lane per key block and lane group"
    ntile = bias_tiles.shape[1]
    qcol, kcol, vcol = (C_COL0 // LANES, (C_COL0 + C_WIDTH) // LANES, (C_COL0 + 2 * C_WIDTH) // LANES)
    npair = C_HEADS // 2
    qchunk = min(2048, t)
    qx = pl.pallas_call(
        functools.partial(_moba_select_body, nb=nb, topk=min(C_TOPK, nb), qchunk=qchunk),
        grid=(bsz, npair),
        in_specs=[
            pl.BlockSpec((1, t, LANES), lambda b, h: (b, 0, qcol + h)),
            pl.BlockSpec((1, t, LANES), lambda b, h: (b, 0, kcol + h)),
        ],
        out_specs=pl.BlockSpec((1, 2, t, LANES), lambda b, h: (b, h, 0, 0)),
        out_shape=jax.ShapeDtypeStruct((bsz, C_HEADS, t, LANES), BF16),
        scratch_shapes=[pltpu.VMEM((LANES // 4, LANES), F32)],
        compiler_params=pltpu.CompilerParams(
            dimension_semantics=("arbitrary", "arbitrary"), vmem_limit_bytes=VMEM_LIMIT),
        name="moba_select",
    )(proj, proj)
    return pl.pallas_call(
        functools.partial(_moba_body, nb=nb, ntile=ntile, unroll=unroll),
        grid=(bsz, npair, nb),
        in_specs=[
            pl.BlockSpec((1, 2, L, LANES), lambda b, h, i: (b, h, i, 0)),
            pl.BlockSpec((1, t, LANES), lambda b, h, i: (b, 0, kcol + h)),
            pl.BlockSpec((1, t, LANES), lambda b, h, i: (b, 0, vcol + h)),
            pl.BlockSpec((2, ntile, L, L), lambda b, h, i: (h, 0, 0, 0)),
        ],
        out_specs=pl.BlockSpec((1, L, LANES), lambda b, h, i: (b, i, h)),
        out_shape=jax.ShapeDtypeStruct((bsz, t, C_WIDTH), F32),
        scratch_shapes=[
            pltpu.VMEM((nb + 1, 2, L, LANES), BF16),
            pltpu.VMEM((nb, 2, VT_ROWS, L), BF16),
            pltpu.VMEM((unroll, 2, L, L), F32),
            pltpu.VMEM((unroll, 2, L, L), BF16),
            pltpu.VMEM((2, VT_ROWS, L), F32),
        ],
        compiler_params=pltpu.CompilerParams(
            dimension_semantics=("arbitrary", "arbitrary", "arbitrary"),
            vmem_limit_bytes=VMEM_LIMIT),
        name="moba",
    )(qx, proj, proj, bias_tiles)


def _post_body(x_ref, ya_ref, yb_ref, yc_ref, p_ref,
               wout_ref, nffn_ref, wup_ref, cw_ref, cb_ref, wdown_ref, nple_ref, wpg_ref, wpe_ref,
               nfin_ref, out_ref, carry_ref, *, final, fch):
    tm = x_ref.shape[1]

    @pl.when(pl.program_id(1) == 0)
    def _():
        carry_ref[...] = jnp.zeros_like(carry_ref)

    x = x_ref[0]
    x = x + jnp.dot(ya_ref[0].astype(BF16), wout_ref[0:A_WIDTH, :], preferred_element_type=F32)
    x = x + jnp.dot(yb_ref[0].astype(BF16), wout_ref[A_WIDTH:A_WIDTH + B_V_WIDTH, :],
                    preferred_element_type=F32)
    x = x + jnp.dot(yc_ref[0].astype(BF16), wout_ref[A_WIDTH + B_V_WIDTH:, :],
                    preferred_element_type=F32)

    h = _rms(x, nffn_ref[...]).astype(BF16)
    row = lax.broadcasted_iota(jnp.int32, (tm, 1), 0)

    def conv_cols(c0):
        cols = slice(c0, c0 + fch)
        u = jnp.dot(h, wup_ref[:, cols], preferred_element_type=F32)
        prev = carry_ref[:, cols]
        carry_ref[:, cols] = u[tm - 8:tm]
        u1 = jnp.where(row == 0, prev[7:8], pltpu.roll(u, 1, 0))
        u2 = jnp.where(row == 0, prev[6:7], jnp.where(row == 1, prev[7:8], pltpu.roll(u, 2, 0)))
        w = cw_ref[:, cols]
        return w[0:1] * u2 + w[1:2] * u1 + w[2:3] * u + cb_ref[:, cols]

    acc = jnp.zeros(x.shape, F32)
    for j in range(D_FF // fch):
        ug = conv_cols(j * fch)
        uv = conv_cols(D_FF + j * fch)
        act = (ug * _sigmoid(ug)) * uv
        acc = acc + jnp.dot(act.astype(BF16), wdown_ref[j * fch:(j + 1) * fch, :],
                            preferred_element_type=F32)
    x = x + acc

    hp = _rms(x, nple_ref[...]).astype(BF16)
    gate = _sigmoid(jnp.dot(hp, wpg_ref[...], preferred_element_type=F32))
    x = x + gate * jnp.dot(p_ref[0].astype(BF16), wpe_ref[...], preferred_element_type=F32)
    if final:
        x = _rms(x, nfin_ref[...])
    out_ref[0] = x


def _post(x, ya, yb, yc, p, wout, nffn, wup, cw, cb, wdown, nple, wpg, wpe, nfin, final,
          tm=512, fch=256):
    bsz, t, d = x.shape
    assert t % tm == 0 and D_FF % fch == 0

    def tok(width):
        return pl.BlockSpec((1, tm, width), lambda b, i: (b, i, 0))

    def const(shape):
        return pl.BlockSpec(shape, lambda b, i: (0,) * len(shape), pipeline_mode=pl.Buffered(1))

    return pl.pallas_call(
        functools.partial(_post_body, final=final, fch=fch),
        grid=(bsz, t // tm),
        in_specs=[tok(d), tok(A_WIDTH), tok(B_V_WIDTH), tok(C_WIDTH), tok(PLE_DIM),
                  const(wout.shape), const(nffn.shape), const(wup.shape), const(cw.shape),
                  const(cb.shape), const(wdown.shape), const(nple.shape), const(wpg.shape),
                  const(wpe.shape), const(nfin.shape)],
        out_specs=tok(d),
        out_shape=jax.ShapeDtypeStruct((bsz, t, d), F32),
        scratch_shapes=[pltpu.VMEM((8, 2 * D_FF), F32)],
        compiler_params=pltpu.CompilerParams(
            dimension_semantics=("arbitrary", "arbitrary"), vmem_limit_bytes=VMEM_LIMIT),
        name="post",
    )(x, ya, yb, yc, p, wout, nffn, wup, cw, cb, wdown, nple, wpg, wpe, nfin)


def _dilated_bias(bias_a, dilation):
    L = A_BLOCK
    m = jnp.arange(3 * L - 1)
    u = bias_a[:, _rel_bucket((2 * L - 1 - m) * dilation)]
    return _toeplitz(u, L, 2 * L)


def _moba_bias(bias_c, t):
    L = C_BLOCK
    nb = t // L
    dist = np.arange(0, t)
    bk = _rel_bucket_np(dist)
    nfar = nb
    for dblk in range(nb, 0, -1):
        if np.all(bk[max(dblk * L - (L - 1), 0):] == bk[-1]):
            nfar = dblk
    m = jnp.arange(2 * L - 1)
    tiles = []
    for dblk in range(nfar):
        u = bias_c[:, _rel_bucket(dblk * L + m - (L - 1))]
        tiles.append(_toeplitz(u, L, L))
    far = jnp.broadcast_to(bias_c[:, int(bk[-1])][:, None, None], (bias_c.shape[0], L, L))
    tiles.append(far)
    kj = jnp.arange(L)[:, None]
    qi = jnp.arange(L)[None, :]
    tiles[0] = jnp.where(qi >= kj, tiles[0] * LOG2E, NEG_BIG)
    tiles[1:] = [tl * LOG2E for tl in tiles[1:]]
    return jnp.stack(tiles, axis=1)


def kernel(x, p, rel_bias, norm_mix, w_in, w_out, lower_bounds, hgrn_gnorm, norm_ffn, w_up, conv_w,
           conv_b, w_down, norm_ple, w_pe, w_pg, norm_final):
    bsz, t, d = x.shape
    depth = w_in.shape[0]
    lb_sm = jax.nn.softmax(lower_bounds.astype(F32), axis=0)
    lb_all = jnp.cumsum(lb_sm, axis=0) - lb_sm[0]
    bias_a = rel_bias[:A_HEADS].astype(F32)
    bias_c = rel_bias[A_HEADS:].astype(F32)
    bias_a_tiles = jnp.stack([_dilated_bias(bias_a, dil) for _, dil in A_PATTERNS])
    bias_c_tiles = _moba_bias(bias_c, t)

    a_end = 3 * A_WIDTH
    b_end = a_end + B_GROUP_WIDTH
    for i in range(depth):
        w = w_in[i]
        w_perm = jnp.concatenate([w[:, a_end:b_end], w[:, :a_end], w[:, b_end:]], axis=1).astype(BF16)
        proj = _inproj(x.reshape(bsz * t, d), norm_mix[i][None], w_perm).reshape(bsz, t, IN_WIDTH)
        ya = _dilated(proj, bias_a_tiles)
        yb = _hgrn(proj, lb_all[i][None], hgrn_gnorm[i].astype(F32)[None])
        yc = _moba(proj, bias_c_tiles)
        x = _post(x, ya, yb, yc, p[i], w_out[i].astype(BF16), norm_ffn[i][None],
                  w_up[i].astype(BF16), conv_w[i], conv_b[i][None], w_down[i].astype(BF16),
                  norm_ple[i][None], w_pg[i].astype(BF16), w_pe[i].astype(BF16), norm_final[None],
                  final=(i == depth - 1))
    return x
```

```python
import functools
import math

import numpy as np
import jax
import jax.numpy as jnp
from jax import lax
from jax.experimental import pallas as pl
from jax.experimental.pallas import tpu as pltpu

F32 = jnp.float32
BF16 = jnp.bfloat16

HEAD_DIM = 64
A_HEADS = 6
A_PATTERNS = ((128, 1), (512, 4), (2048, 16))
A_BLOCK = 128
B_HEADS = 4
B_KEY_DIM = 128
B_VAL_DIM = 64
C_HEADS = 6
C_BLOCK = 256
C_TOPK = 3
N_BUCKETS = 32
MAX_DISTANCE = 2048
D_FF = 2816
PLE_DIM = 256
EPS = 1e-6
NEG_BIG = -1e30

A_WIDTH = A_HEADS * HEAD_DIM
B_QK_WIDTH = B_HEADS * B_KEY_DIM
B_V_WIDTH = B_HEADS * B_VAL_DIM
C_WIDTH = C_HEADS * HEAD_DIM
IN_WIDTH = 3 * A_WIDTH + 2 * B_QK_WIDTH + 2 * B_V_WIDTH + 3 * C_WIDTH
LANES = 128

B_GROUP_WIDTH = 2 * B_QK_WIDTH + 2 * B_V_WIDTH
A_COL0 = B_GROUP_WIDTH
C_COL0 = A_COL0 + 3 * A_WIDTH

LOG2E = 1.4426950408889634
VT_ROWS = HEAD_DIM + 16

HGRN_CHUNK = 64
HGRN_SUB = 8
HGRN_FSUB = 16
HGRN_FACTOR_LIMIT = 64.0
VMEM_LIMIT = 56 * 1024 * 1024

_NT = (((1,), (1,)), ((), ()))


def _rel_bucket(dist):
    max_exact = N_BUCKETS // 2
    d = jnp.maximum(dist, 0)
    df = jnp.maximum(d, max_exact).astype(F32)
    large = max_exact + (jnp.log(df / max_exact) / math.log(MAX_DISTANCE / max_exact)
                         * (N_BUCKETS - max_exact)).astype(jnp.int32)
    large = jnp.minimum(large, N_BUCKETS - 1)
    return jnp.where(d < max_exact, d, large)


def _rel_bucket_np(dist):
    max_exact = N_BUCKETS // 2
    d = np.maximum(dist, 0)
    df = np.maximum(d, max_exact).astype(np.float64)
    large = max_exact + (np.log(df / max_exact) / math.log(MAX_DISTANCE / max_exact)
                         * (N_BUCKETS - max_exact)).astype(np.int64)
    large = np.minimum(large, N_BUCKETS - 1)
    return np.where(d < max_exact, d, large)


def _toeplitz(u, rows, cols):
    n = rows + cols - 1
    w = jnp.concatenate([u, jnp.zeros(u.shape[:-1] + (1,), u.dtype)], axis=-1)
    flat = jnp.tile(w, rows)[..., : rows * n]
    m = flat.reshape(u.shape[:-1] + (rows, n))
    return m[..., rows - 1: rows - 1 + cols]


def _rms(x, g):
    return x * lax.rsqrt(jnp.mean(x * x, axis=-1, keepdims=True) + EPS) * g


def _sigmoid(x):
    return 1.0 / (1.0 + jnp.exp(-x))


def _inproj_body(x_ref, g_ref, w_ref, o_ref, *, nch):
    h = _rms(x_ref[...], g_ref[...]).astype(BF16)
    for j in range(0, w_ref.shape[1], nch):
        o_ref[:, j:j + nch] = jnp.dot(h, w_ref[:, j:j + nch], preferred_element_type=F32)


def _inproj(x2, gain, w_bf, tm=512, nch=768):
    n, d = x2.shape
    width = w_bf.shape[1]
    return pl.pallas_call(
        functools.partial(_inproj_body, nch=nch),
        grid=(n // tm,),
        in_specs=[
            pl.BlockSpec((tm, d), lambda i: (i, 0)),
            pl.BlockSpec((1, d), lambda i: (0, 0)),
            pl.BlockSpec((d, width), lambda i: (0, 0), pipeline_mode=pl.Buffered(1)),
        ],
        out_specs=pl.BlockSpec((tm, width), lambda i: (i, 0)),
        out_shape=jax.ShapeDtypeStruct((n, width), F32),
        compiler_params=pltpu.CompilerParams(
            dimension_semantics=("arbitrary",), vmem_limit_bytes=VMEM_LIMIT),
        name="inproj",
    )(x2, gain, w_bf)


def _dilated_body(q_ref, kp_ref, k_ref, vp_ref, v_ref, bias_ref, o_ref, m_sc, l_sc, acc_sc,
                  *, span, patterns, unroll):
    L = A_BLOCK
    first_span = pl.program_id(2) == 0
    lo = lax.broadcasted_iota(jnp.int32, (L, LANES), 1) < HEAD_DIM
    scale = (HEAD_DIM ** -0.5) * LOG2E
    last = len(patterns) - 1

    for pi, (window, d) in enumerate(patterns):
        nblk = span // (L * d)

        def rows_at(start, d=d):
            return pl.ds(start, L) if d == 1 else pl.ds(start, L, stride=d)

        def unit(u, carry, pi=pi, d=d, nblk=nblk, rows_at=rows_at):
            r = u % d
            n = u // d
            rows = rows_at(r + d * L * n)
            rows_prev_own = rows_at(r + d * L * jnp.maximum(n - 1, 0))
            rows_prev_span = rows_at(r + d * L * (nblk - 1))
            from_prev_span = n == 0
            variant = (from_prev_span & first_span).astype(jnp.int32)
            q2 = q_ref[rows, :] * scale
            kp = jnp.where(from_prev_span, kp_ref[rows_prev_span, :], k_ref[rows_prev_own, :])
            vp = jnp.where(from_prev_span, vp_ref[rows_prev_span, :], v_ref[rows_prev_own, :])
            k2 = jnp.concatenate([kp, k_ref[rows, :]], axis=0).astype(BF16)
            v2 = jnp.concatenate([vp, v_ref[rows, :]], axis=0).astype(BF16)
            res = []
            for half in range(2):
                hmask = lo if half == 0 else jnp.logical_not(lo)
                qm = jnp.where(hmask, q2, 0.0).astype(BF16)
                s = lax.dot_general(qm, k2, _NT, preferred_element_type=F32)
                s = s + bias_ref[pi, variant, half]
                m = jnp.max(s, axis=-1, keepdims=True)
                p = jnp.exp2(s - m)
                res.append((m, jnp.sum(p, axis=-1, keepdims=True),
                            jnp.dot(p.astype(BF16), v2, preferred_element_type=F32)))
            m_b = jnp.where(lo, res[0][0], res[1][0])
            l_b = jnp.where(lo, res[0][1], res[1][1])
            acc_b = jnp.where(lo, res[0][2], res[1][2])
            if pi > 0:
                m_old = m_sc[rows, :]
                m_new = jnp.maximum(m_old, m_b)
                w_old = jnp.exp2(m_old - m_new)
                w_b = jnp.exp2(m_b - m_new)
                l_b = w_old * l_sc[rows, :] + w_b * l_b
                acc_b = w_old * acc_sc[rows, :] + w_b * acc_b
                m_b = m_new
            if pi < last:
                m_sc[rows, :] = m_b
                l_sc[rows, :] = l_b
                acc_sc[rows, :] = acc_b
            else:
                o_ref[rows, :] = acc_b / l_b
            return carry

        lax.fori_loop(0, span // L, unit, 0, unroll=unroll)


def _dilated(proj, bias, patterns=A_PATTERNS, unroll=8):
    bsz, t, _ = proj.shape
    L = A_BLOCK
    span = L * max(d for _, d in patterns)
    assert t % span == 0
    npair = A_WIDTH // LANES
    qc, kc, vc = A_COL0 // LANES, A_COL0 // LANES + npair, A_COL0 // LANES + 2 * npair

    def own(c):
        return pl.BlockSpec((None, span, LANES), lambda b, h, n: (b, n, c + h))

    def prev(c):
        return pl.BlockSpec((None, span, LANES), lambda b, h, n: (b, jnp.maximum(n - 1, 0), c + h))

    return pl.pallas_call(
        functools.partial(_dilated_body, span=span, patterns=patterns, unroll=unroll),
        grid=(bsz, npair, t // span),
        in_specs=[own(qc), prev(kc), own(kc), prev(vc), own(vc),
                  pl.BlockSpec((len(patterns), 2, 2, L, 2 * L), lambda b, h, n: (0, 0, h, 0, 0))],
        out_specs=pl.BlockSpec((None, span, LANES), lambda b, h, n: (b, n, h)),
        out_shape=jax.ShapeDtypeStruct((bsz, t, A_WIDTH), F32),
        scratch_shapes=[pltpu.VMEM((span, LANES), F32)] * 3,
        compiler_params=pltpu.CompilerParams(
            dimension_semantics=("arbitrary", "arbitrary", "arbitrary"), vmem_limit_bytes=VMEM_LIMIT),
        name="dilated",
    )(proj, proj, proj, proj, proj, bias)


def _cumsum_rows(tri, x):
    hi = x.astype(BF16)
    r1 = x - hi.astype(F32)
    mid = r1.astype(BF16)
    lo = (r1 - mid.astype(F32)).astype(BF16)
    return sum(jnp.dot(tri, part, preferred_element_type=F32) for part in (hi, mid, lo))


def _hgrn_gates(z, lb):
    sig = _sigmoid(z)
    fg = lb + (1.0 - lb) * sig
    return jnp.log(jnp.maximum(fg, 1e-30)) * LOG2E, (1.0 - lb) * (1.0 - sig)


def _hgrn_intra_exact(q, k, b, v, lag, sub):
    C, V = v.shape
    amat = jnp.where(lag == 0, jnp.sum(q * k, axis=-1, keepdims=True), 0.0)
    for d in range(1, sub):
        ks = pltpu.roll(k, d, 0)
        bs = pltpu.roll(b, d, 0)
        pr = q * ks * jnp.exp2(jnp.minimum(b - bs, 0.0))
        amat = jnp.where(lag == d, jnp.sum(pr, axis=-1, keepdims=True), amat)
    o = jnp.dot(amat.astype(BF16), v.astype(BF16), preferred_element_type=F32)
    for j in range(C // sub - 1):
        e = sub * (j + 1)
        r = b[e - 1:e]
        kh = (k[e - sub:e] * jnp.exp2(r - b[e - sub:e])).astype(BF16)
        qt = (q[e:] * jnp.exp2(b[e:] - r)).astype(BF16)
        a = lax.dot_general(qt, kh, _NT, preferred_element_type=F32)
        contrib = jnp.dot(a.astype(BF16), v[e - sub:e].astype(BF16), preferred_element_type=F32)
        o = o + jnp.concatenate([jnp.zeros((e, V), F32), contrib], axis=0)
    return o


def _hgrn_intra_factored(q, k, b, v, causal, sub):
    C, K = q.shape
    qs, ks = [], []
    for j in range(C // sub):
        s0 = sub * j
        r = b[s0 - 1:s0] if j > 0 else jnp.zeros((1, K), F32)
        qt = (q[s0:] * jnp.exp2(b[s0:] - r)).astype(BF16)
        kh = (k[s0:s0 + sub] * jnp.exp2(r - b[s0:s0 + sub])).astype(BF16)
        above = [jnp.zeros((s0, K), BF16)] if s0 > 0 else []
        below = [jnp.zeros((C - s0 - sub, K), BF16)] if C - s0 - sub > 0 else []
        qs.append(jnp.concatenate(above + [qt], axis=0))
        ks.append(jnp.concatenate(above + [kh] + below, axis=0))
    a = lax.dot_general(jnp.concatenate(qs, axis=1), jnp.concatenate(ks, axis=1), _NT,
                        preferred_element_type=F32)
    a = jnp.where(causal, a, 0.0)
    return jnp.dot(a.astype(BF16), v.astype(BF16), preferred_element_type=F32)


def _hgrn_body(q_ref, f_ref, i_ref, g_ref, lb_ref, gn_ref, o_ref, st_ref, *, chunk, sub, fsub, nchunk):
    C = chunk
    K, V = B_KEY_DIM, B_VAL_DIM

    @pl.when(pl.program_id(1) == 0)
    def _():
        st_ref[...] = jnp.zeros_like(st_ref)

    ri = lax.broadcasted_iota(jnp.int32, (C, C), 0)
    ci = lax.broadcasted_iota(jnp.int32, (C, C), 1)
    causal = ri >= ci
    tri = causal.astype(F32).astype(BF16)
    lag = jnp.where((ri // sub == ci // sub) & causal, ri - ci, -1)

    def run(factored):
        def chunk_body(ic, carry):
            r0 = pl.multiple_of(ic * C, C)
            rows = pl.ds(r0, C)
            outs = []
            for h in range(B_HEADS):
                kc = slice(h * K, (h + 1) * K)
                vc = slice(h * V, (h + 1) * V)
                qraw = q_ref[0, rows, kc]
                v = i_ref[0, rows, vc]
                gate = g_ref[0, rows, vc]
                g2, k = _hgrn_gates(f_ref[0, rows, kc], lb_ref[:, kc])
                q = qraw * _sigmoid(qraw)
                b = _cumsum_rows(tri, g2)
                st = st_ref[h]
                o = lax.dot_general((q * jnp.exp2(b)).astype(BF16), st.astype(BF16), _NT,
                                    preferred_element_type=F32)
                if factored:
                    o = o + _hgrn_intra_factored(q, k, b, v, causal, fsub)
                else:
                    o = o + _hgrn_intra_exact(q, k, b, v, lag, sub)
                bl = b[C - 1:C]
                kd = (k * jnp.exp2(bl - b)).astype(BF16)
                st_ref[h] = st * jnp.exp2(bl) + jnp.dot(v.T.astype(BF16), kd, preferred_element_type=F32)
                on = _rms(o, gn_ref[...])
                outs.append(on * (gate * _sigmoid(gate)))
            o_ref[0, rows, :] = jnp.concatenate(outs, axis=-1)
            return carry

        lax.fori_loop(0, nchunk, chunk_body, 0, unroll=4 if factored else 1)

    g2_all, _ = _hgrn_gates(f_ref[0], lb_ref[...])
    tb = g2_all.shape[0]
    worst = jnp.max(-jnp.sum(g2_all.reshape(tb // fsub, fsub, g2_all.shape[1]), axis=1))
    lax.cond(worst < HGRN_FACTOR_LIMIT, lambda: run(True), lambda: run(False))


def _hgrn(proj, lb, gnorm, tb=256):
    bsz, t, _ = proj.shape
    C = HGRN_CHUNK
    assert t % tb == 0 and tb % C == 0
    kw, vw = B_QK_WIDTH, B_V_WIDTH
    return pl.pallas_call(
        functools.partial(_hgrn_body, chunk=C, sub=HGRN_SUB, fsub=HGRN_FSUB, nchunk=tb // C),
        grid=(bsz, t // tb),
        in_specs=[
            pl.BlockSpec((1, tb, kw), lambda b, i: (b, i, 0)),
            pl.BlockSpec((1, tb, kw), lambda b, i: (b, i, 1)),
            pl.BlockSpec((1, tb, vw), lambda b, i: (b, i, 2 * kw // vw)),
            pl.BlockSpec((1, tb, vw), lambda b, i: (b, i, 2 * kw // vw + 1)),
            pl.BlockSpec((1, kw), lambda b, i: (0, 0)),
            pl.BlockSpec((1, B_VAL_DIM), lambda b, i: (0, 0)),
        ],
        out_specs=pl.BlockSpec((1, tb, vw), lambda b, i: (b, i, 0)),
        out_shape=jax.ShapeDtypeStruct((bsz, t, vw), F32),
        scratch_shapes=[pltpu.VMEM((B_HEADS, B_VAL_DIM, B_KEY_DIM), F32)],
        compiler_params=pltpu.CompilerParams(
            dimension_semantics=("arbitrary", "arbitrary"), vmem_limit_bytes=VMEM_LIMIT),
        name="hgrn2",
    )(proj, proj, proj, proj, lb, gnorm)


def _moba_select_body(q_ref, k_ref, qx_ref, kmean, *, nb, topk, qchunk):
    L = C_BLOCK
    E = HEAD_DIM
    G = LANES // 4
    kmean[...] = jnp.zeros_like(kmean)
    for n in range(nb):
        kmean[n:n + 1, :] = jnp.mean(k_ref[0, n * L:(n + 1) * L, :], axis=0, keepdims=True)
    lane = lax.broadcasted_iota(jnp.int32, (qchunk, LANES), 1)
    grp = lane // G
    blk_i = lax.broadcasted_iota(jnp.int32, (G, qchunk), 0)
    blk_f = blk_i.astype(F32)
    qcol = lax.broadcasted_iota(jnp.int32, (G, qchunk), 1)
    place_r = lax.broadcasted_iota(jnp.int32, (G, LANES), 0)
    place_c = lax.broadcasted_iota(jnp.int32, (G, LANES), 1)
    qscale = (E ** -0.5) * LOG2E

    def chunk_body(c, carry):
        r0 = pl.multiple_of(c * qchunk, qchunk)
        q2 = q_ref[0, pl.ds(r0, qchunk), :]
        qblk = (qcol + r0) // L
        for half in range(2):
            hmask = (lane < E) if half == 0 else (lane >= E)
            qf = jnp.where(hmask, q2, 0.0)
            gate = lax.dot_general(kmean[...], qf, _NT, preferred_element_type=F32,
                                   precision=lax.Precision.HIGHEST)
            gate = jnp.where(blk_i < qblk, gate, NEG_BIG)
            picked = jnp.zeros((G, qchunk), jnp.bool_)
            for _ in range(topk):
                mx = jnp.max(gate, axis=0, keepdims=True)
                ix = jnp.min(jnp.where(gate == mx, blk_f, float(G)), axis=0, keepdims=True)
                hit = blk_f == ix
                picked = picked | (hit & (mx > 0.5 * NEG_BIG))
                gate = jnp.where(hit, -jnp.inf, gate)
            wanted = picked | (blk_i == qblk)
            sel_grp = 2 if half == 0 else 0
            place = (place_c == place_r + sel_grp * G).astype(BF16)
            flags = jnp.dot(wanted.astype(F32).T.astype(BF16), place, preferred_element_type=F32)
            ext = jnp.where((grp == sel_grp + 1) | ((grp == sel_grp) & (flags < 0.5)), NEG_BIG, 0.0)
            qx_ref[0, half, pl.ds(r0, qchunk), :] = jnp.where(hmask, qf * qscale, ext).astype(BF16)
        return carry

    lax.fori_loop(0, q_ref.shape[1] // qchunk, chunk_body, 0)


def _moba_body(qx, k_ref, v_ref, bias_ref, o_ref, kbh, vtb, sbuf, pbuf, acc,
               *, nb, ntile, unroll):
    L = C_BLOCK
    E = HEAD_DIM
    G = LANES // 4
    U = unroll
    LEAD = 1
    assert U >= 2 * LEAD
    qb = pl.program_id(2)
    lane = lax.broadcasted_iota(jnp.int32, (L, LANES), 1)

    @pl.when(qb == 0)
    def _():
        ones_rows = jnp.ones((VT_ROWS - E, L), BF16)
        for n in range(nb):
            blk = k_ref[0, n * L:(n + 1) * L, :]
            kbh[n, 0] = jnp.where(lane < E, blk, (lane == 2 * G + n).astype(F32)).astype(BF16)
            kbh[n, 1] = jnp.where(lane >= E, blk, (lane == n).astype(F32)).astype(BF16)
            vt = v_ref[0, n * L:(n + 1) * L, :].T.astype(BF16)
            for half in range(2):
                vtb[n, half, 0:E, :] = vt[half * E:(half + 1) * E, :]
                vtb[n, half, E:, :] = ones_rows
        kbh[nb, 0] = (lane == 3 * G).astype(BF16)
        kbh[nb, 1] = (lane == G).astype(BF16)

    def scores_stage(dist, slot_):
        kb = jnp.where(dist <= qb, qb - dist, nb)
        tile = jnp.minimum(dist, ntile - 1)
        for half in range(2):
            s = lax.dot_general(kbh[kb, half], qx[0, half], _NT, preferred_element_type=F32)
            sbuf[slot_, half] = s + bias_ref[half, tile]

    def softmax_stage(slot_, m_old):
        out = []
        for half in range(2):
            s = sbuf[slot_, half]
            m_new = jnp.maximum(m_old[half], jnp.max(s, axis=0, keepdims=True))
            pbuf[slot_, half] = jnp.exp2(s - m_new).astype(BF16)
            out.append(m_new)
        return tuple(out)

    def pv_stage(dist, slot_, alpha):
        kb = jnp.clip(qb - dist, 0, nb - 1)
        for half in range(2):
            acc[half] = alpha[half] * acc[half] + jnp.dot(vtb[kb, half], pbuf[slot_, half],
                                                          preferred_element_type=F32)

    pbuf[...] = jnp.zeros_like(pbuf)
    acc[...] = jnp.zeros_like(acc)
    for d0 in range(LEAD):
        scores_stage(d0, d0 % U)

    def trip(j, carry):
        m_old, alphas = carry[0:2], list(carry[2:])
        for u in range(U):
            dist = j * U + u
            pv_stage(dist - LEAD, (u - LEAD) % U, alphas[0:2])
            m_new = softmax_stage(u, m_old)
            scores_stage(dist + LEAD, (u + LEAD) % U)
            alphas = alphas[2:] + [jnp.exp2(a - b) for a, b in zip(m_old, m_new)]
            m_old = m_new
        return tuple(m_old) + tuple(alphas)

    row = jnp.full((1, L), NEG_BIG, F32)
    one = jnp.ones((1, L), F32)
    ntrips = (qb + U) // U
    carry = lax.fori_loop(0, ntrips, trip, (row, row) + (one,) * (2 * LEAD))
    for t in range(LEAD):
        pv_stage(ntrips * U - LEAD + t, (t - LEAD) % U, carry[2 + 2 * t:4 + 2 * t])
    out_t = jnp.concatenate([acc[half, 0:E, :] / acc[half, E:E + 1, :] for half in range(2)], axis=0)
    o_ref[0] = out_t.T


def _moba(proj, bias_tiles, unroll=2):
    bsz, t, _ = proj.shape
    L = C_BLOCK
    assert t % L == 0
    nb = t // L
    assert nb <= LANES // 4, "one selection lane per key block and lane group"
    ntile = bias_tiles.shape[1]
    qcol, kcol, vcol = (C_COL0 // LANES, (C_COL0 + C_WIDTH) // LANES, (C_COL0 + 2 * C_WIDTH) // LANES)
    npair = C_HEADS // 2
    qchunk = min(2048, t)
    assert t % qchunk == 0
    qx = pl.pallas_call(
        functools.partial(_moba_select_body, nb=nb, topk=min(C_TOPK, nb), qchunk=qchunk),
        grid=(bsz, npair),
        in_specs=[
            pl.BlockSpec((1, t, LANES), lambda b, h: (b, 0, qcol + h)),
            pl.BlockSpec((1, t, LANES), lambda b, h: (b, 0, kcol + h)),
        ],
        out_specs=pl.BlockSpec((1, 2, t, LANES), lambda b, h: (b, h, 0, 0)),
        out_shape=jax.ShapeDtypeStruct((bsz, C_HEADS, t, LANES), BF16),
        scratch_shapes=[pltpu.VMEM((LANES // 4, LANES), F32)],
        compiler_params=pltpu.CompilerParams(
            dimension_semantics=("arbitrary", "arbitrary"), vmem_limit_bytes=VMEM_LIMIT),
        name="moba_select",
    )(proj, proj)
    return pl.pallas_call(
        functools.partial(_moba_body, nb=nb, ntile=ntile, unroll=unroll),
        grid=(bsz, npair, nb),
        in_specs=[
            pl.BlockSpec((1, 2, L, LANES), lambda b, h, i: (b, h, i, 0)),
            pl.BlockSpec((1, t, LANES), lambda b, h, i: (b, 0, kcol + h)),
            pl.BlockSpec((1, t, LANES), lambda b, h, i: (b, 0, vcol + h)),
            pl.BlockSpec((2, ntile, L, L), lambda b, h, i: (h, 0, 0, 0)),
        ],
        out_specs=pl.BlockSpec((1, L, LANES), lambda b, h, i: (b, i, h)),
        out_shape=jax.ShapeDtypeStruct((bsz, t, C_WIDTH), F32),
        scratch_shapes=[
            pltpu.VMEM((nb + 1, 2, L, LANES), BF16),
            pltpu.VMEM((nb, 2, VT_ROWS, L), BF16),
            pltpu.VMEM((unroll, 2, L, L), F32),
            pltpu.VMEM((unroll, 2, L, L), BF16),
            pltpu.VMEM((2, VT_ROWS, L), F32),
        ],
        compiler_params=pltpu.CompilerParams(
            dimension_semantics=("arbitrary", "arbitrary", "arbitrary"),
            vmem_limit_bytes=VMEM_LIMIT),
        name="moba",
    )(qx, proj, proj, bias_tiles)


def _post_body(x_ref, ya_ref, yb_ref, yc_ref, p_ref,
               wout_ref, nffn_ref, wup_ref, cw_ref, cb_ref, wdown_ref, nple_ref, wpg_ref, wpe_ref,
               nfin_ref, out_ref, carry_ref, *, final, fch):
    tm = x_ref.shape[1]

    @pl.when(pl.program_id(1) == 0)
    def _():
        carry_ref[...] = jnp.zeros_like(carry_ref)

    x = x_ref[0]
    x = x + jnp.dot(ya_ref[0].astype(BF16), wout_ref[0:A_WIDTH, :], preferred_element_type=F32)
    x = x + jnp.dot(yb_ref[0].astype(BF16), wout_ref[A_WIDTH:A_WIDTH + B_V_WIDTH, :],
                    preferred_element_type=F32)
    x = x + jnp.dot(yc_ref[0].astype(BF16), wout_ref[A_WIDTH + B_V_WIDTH:, :],
                    preferred_element_type=F32)

    h = _rms(x, nffn_ref[...]).astype(BF16)
    row = lax.broadcasted_iota(jnp.int32, (tm, 1), 0)

    def conv_cols(c0):
        cols = slice(c0, c0 + fch)
        u = jnp.dot(h, wup_ref[:, cols], preferred_element_type=F32)
        prev = carry_ref[:, cols]
        carry_ref[:, cols] = u[tm - 8:tm]
        u1 = jnp.where(row == 0, prev[7:8], pltpu.roll(u, 1, 0))
        u2 = jnp.where(row == 0, prev[6:7], jnp.where(row == 1, prev[7:8], pltpu.roll(u, 2, 0)))
        w = cw_ref[:, cols]
        return w[0:1] * u2 + w[1:2] * u1 + w[2:3] * u + cb_ref[:, cols]

    acc = jnp.zeros(x.shape, F32)
    for j in range(D_FF // fch):
        ug = conv_cols(j * fch)
        uv = conv_cols(D_FF + j * fch)
        act = (ug * _sigmoid(ug)) * uv
        acc = acc + jnp.dot(act.astype(BF16), wdown_ref[j * fch:(j + 1) * fch, :],
                            preferred_element_type=F32)
    x = x + acc

    hp = _rms(x, nple_ref[...]).astype(BF16)
    gate = _sigmoid(jnp.dot(hp, wpg_ref[...], preferred_element_type=F32))
    x = x + gate * jnp.dot(p_ref[0].astype(BF16), wpe_ref[...], preferred_element_type=F32)
    if final:
        x = _rms(x, nfin_ref[...])
    out_ref[0] = x


def _post(x, ya, yb, yc, p, wout, nffn, wup, cw, cb, wdown, nple, wpg, wpe, nfin, final,
          tm=512, fch=256):
    bsz, t, d = x.shape
    assert t % tm == 0 and D_FF % fch == 0

    def tok(width):
        return pl.BlockSpec((1, tm, width), lambda b, i: (b, i, 0))

    def const(shape):
        return pl.BlockSpec(shape, lambda b, i: (0,) * len(shape), pipeline_mode=pl.Buffered(1))

    return pl.pallas_call(
        functools.partial(_post_body, final=final, fch=fch),
        grid=(bsz, t // tm),
        in_specs=[tok(d), tok(A_WIDTH), tok(B_V_WIDTH), tok(C_WIDTH), tok(PLE_DIM),
                  const(wout.shape), const(nffn.shape), const(wup.shape), const(cw.shape),
                  const(cb.shape), const(wdown.shape), const(nple.shape), const(wpg.shape),
                  const(wpe.shape), const(nfin.shape)],
        out_specs=tok(d),
        out_shape=jax.ShapeDtypeStruct((bsz, t, d), F32),
        scratch_shapes=[pltpu.VMEM((8, 2 * D_FF), F32)],
        compiler_params=pltpu.CompilerParams(
            dimension_semantics=("arbitrary", "arbitrary"), vmem_limit_bytes=VMEM_LIMIT),
        name="post",
    )(x, ya, yb, yc, p, wout, nffn, wup, cw, cb, wdown, nple, wpg, wpe, nfin)


def _dilated_bias(bias_a, window, dilation):
    L = A_BLOCK
    m = jnp.arange(3 * L - 1)
    u = bias_a[:, _rel_bucket((2 * L - 1 - m) * dilation)]
    tile = _toeplitz(u, L, 2 * L) * LOG2E
    qi = jnp.arange(L)[:, None]
    kj = jnp.arange(2 * L)[None, :]
    delta = qi + L - kj
    band = (delta >= 0) & (delta <= window // dilation)
    return jnp.stack([jnp.where(band, tile, NEG_BIG), jnp.where(band & (kj >= L), tile, NEG_BIG)])


def _moba_bias(bias_c, t):
    L = C_BLOCK
    nb = t // L
    dist = np.arange(0, t)
    bk = _rel_bucket_np(dist)
    nfar = nb
    for dblk in range(nb, 0, -1):
        if np.all(bk[max(dblk * L - (L - 1), 0):] == bk[-1]):
            nfar = dblk
    m = jnp.arange(2 * L - 1)
    tiles = []
    for dblk in range(nfar):
        u = bias_c[:, _rel_bucket(dblk * L + m - (L - 1))]
        tiles.append(_toeplitz(u, L, L))
    far = jnp.broadcast_to(bias_c[:, int(bk[-1])][:, None, None], (bias_c.shape[0], L, L))
    tiles.append(far)
    kj = jnp.arange(L)[:, None]
    qi = jnp.arange(L)[None, :]
    tiles[0] = jnp.where(qi >= kj, tiles[0] * LOG2E, NEG_BIG)
    tiles[1:] = [tl * LOG2E for tl in tiles[1:]]
    return jnp.stack(tiles, axis=1)


def kernel(x, p, rel_bias, norm_mix, w_in, w_out, lower_bounds, hgrn_gnorm, norm_ffn, w_up, conv_w,
           conv_b, w_down, norm_ple, w_pe, w_pg, norm_final):
    bsz, t, d = x.shape
    depth = w_in.shape[0]
    lb_sm = jax.nn.softmax(lower_bounds.astype(F32), axis=0)
    lb_all = jnp.cumsum(lb_sm, axis=0) - lb_sm[0]
    bias_a = rel_bias[:A_HEADS].astype(F32)
    bias_c = rel_bias[A_HEADS:].astype(F32)
    bias_a_tiles = jnp.stack([_dilated_bias(bias_a, win, dil) for win, dil in A_PATTERNS])
    bias_c_tiles = _moba_bias(bias_c, t)

    a_end = 3 * A_WIDTH
    b_end = a_end + B_GROUP_WIDTH
    for i in range(depth):
        w = w_in[i]
        w_perm = jnp.concatenate([w[:, a_end:b_end], w[:, :a_end], w[:, b_end:]], axis=1).astype(BF16)
        proj = _inproj(x.reshape(bsz * t, d), norm_mix[i][None], w_perm).reshape(bsz, t, IN_WIDTH)
        ya = _dilated(proj, bias_a_tiles)
        yb = _hgrn(proj, lb_all[i][None], hgrn_gnorm[i].astype(F32)[None])
        yc = _moba(proj, bias_c_tiles)
        x = _post(x, ya, yb, yc, p[i], w_out[i].astype(BF16), norm_ffn[i][None],
                  w_up[i].astype(BF16), conv_w[i], conv_b[i][None], w_down[i].astype(BF16),
                  norm_ple[i][None], w_pg[i].astype(BF16), w_pe[i].astype(BF16), norm_final[None],
                  final=(i == depth - 1))
    return x
```

```python
import functools
import math

import numpy as np
import jax
import jax.numpy as jnp
from jax import lax
from jax.experimental import pallas as pl
from jax.experimental.pallas import tpu as pltpu

F32 = jnp.float32
BF16 = jnp.bfloat16

HEAD_DIM = 64
A_HEADS = 6
A_PATTERNS = ((128, 1), (512, 4), (2048, 16))
A_BLOCK = 128
B_HEADS = 4
B_KEY_DIM = 128
B_VAL_DIM = 64
C_HEADS = 6
C_BLOCK = 256
C_TOPK = 3
N_BUCKETS = 32
MAX_DISTANCE = 2048
D_FF = 2816
PLE_DIM = 256
EPS = 1e-6
NEG_BIG = -1e30

A_WIDTH = A_HEADS * HEAD_DIM
B_QK_WIDTH = B_HEADS * B_KEY_DIM
B_V_WIDTH = B_HEADS * B_VAL_DIM
C_WIDTH = C_HEADS * HEAD_DIM
IN_WIDTH = 3 * A_WIDTH + 2 * B_QK_WIDTH + 2 * B_V_WIDTH + 3 * C_WIDTH
LANES = 128

B_GROUP_WIDTH = 2 * B_QK_WIDTH + 2 * B_V_WIDTH
A_COL0 = B_GROUP_WIDTH
C_COL0 = A_COL0 + 3 * A_WIDTH

LOG2E = 1.4426950408889634
VT_ROWS = HEAD_DIM + 16

HGRN_CHUNK = 64
HGRN_SUB = 8
HGRN_FSUB = 16
HGRN_FACTOR_LIMIT = 64.0
VMEM_LIMIT = 56 * 1024 * 1024

_NT = (((1,), (1,)), ((), ()))


def _rel_bucket(dist):
    max_exact = N_BUCKETS // 2
    d = jnp.maximum(dist, 0)
    df = jnp.maximum(d, max_exact).astype(F32)
    large = max_exact + (jnp.log(df / max_exact) / math.log(MAX_DISTANCE / max_exact)
                         * (N_BUCKETS - max_exact)).astype(jnp.int32)
    large = jnp.minimum(large, N_BUCKETS - 1)
    return jnp.where(d < max_exact, d, large)


def _rel_bucket_np(dist):
    max_exact = N_BUCKETS // 2
    d = np.maximum(dist, 0)
    df = np.maximum(d, max_exact).astype(np.float64)
    large = max_exact + (np.log(df / max_exact) / math.log(MAX_DISTANCE / max_exact)
                         * (N_BUCKETS - max_exact)).astype(np.int64)
    large = np.minimum(large, N_BUCKETS - 1)
    return np.where(d < max_exact, d, large)


def _toeplitz(u, rows, cols):
    n = rows + cols - 1
    w = jnp.concatenate([u, jnp.zeros(u.shape[:-1] + (1,), u.dtype)], axis=-1)
    flat = jnp.tile(w, rows)[..., : rows * n]
    m = flat.reshape(u.shape[:-1] + (rows, n))
    return m[..., rows - 1: rows - 1 + cols]


def _rms(x, g):
    return x * lax.rsqrt(jnp.mean(x * x, axis=-1, keepdims=True) + EPS) * g


def _sigmoid(x):
    return 1.0 / (1.0 + jnp.exp(-x))


def _inproj_body(x_ref, g_ref, w_ref, o_ref, *, nch):
    h = _rms(x_ref[...], g_ref[...]).astype(BF16)
    for j in range(0, w_ref.shape[1], nch):
        o_ref[:, j:j + nch] = jnp.dot(h, w_ref[:, j:j + nch], preferred_element_type=F32)


def _inproj(x2, gain, w_bf, tm=512, nch=768):
    n, d = x2.shape
    width = w_bf.shape[1]
    return pl.pallas_call(
        functools.partial(_inproj_body, nch=nch),
        grid=(n // tm,),
        in_specs=[
            pl.BlockSpec((tm, d), lambda i: (i, 0)),
            pl.BlockSpec((1, d), lambda i: (0, 0)),
            pl.BlockSpec((d, width), lambda i: (0, 0), pipeline_mode=pl.Buffered(1)),
        ],
        out_specs=pl.BlockSpec((tm, width), lambda i: (i, 0)),
        out_shape=jax.ShapeDtypeStruct((n, width), F32),
        compiler_params=pltpu.CompilerParams(
            dimension_semantics=("arbitrary",), vmem_limit_bytes=VMEM_LIMIT),
        name="inproj",
    )(x2, gain, w_bf)


def _dilated_body(q_ref, kp_ref, k_ref, vp_ref, v_ref, bias_ref, o_ref, m_sc, l_sc, acc_sc,
                  *, span, patterns, unroll):
    L = A_BLOCK
    first_span = pl.program_id(2) == 0
    lo = lax.broadcasted_iota(jnp.int32, (L, LANES), 1) < HEAD_DIM
    scale = (HEAD_DIM ** -0.5) * LOG2E
    last = len(patterns) - 1

    for pi, (window, d) in enumerate(patterns):
        nblk = span // (L * d)

        def rows_at(start, d=d):
            return pl.ds(start, L) if d == 1 else pl.ds(start, L, stride=d)

        def unit(u, carry, pi=pi, d=d, nblk=nblk, rows_at=rows_at):
            r = u % d
            n = u // d
            rows = rows_at(r + d * L * n)
            rows_prev_own = rows_at(r + d * L * jnp.maximum(n - 1, 0))
            rows_prev_span = rows_at(r + d * L * (nblk - 1))
            from_prev_span = n == 0
            variant = (from_prev_span & first_span).astype(jnp.int32)
            q2 = q_ref[rows, :] * scale
            kp = jnp.where(from_prev_span, kp_ref[rows_prev_span, :], k_ref[rows_prev_own, :])
            vp = jnp.where(from_prev_span, vp_ref[rows_prev_span, :], v_ref[rows_prev_own, :])
            k2 = jnp.concatenate([kp, k_ref[rows, :]], axis=0).astype(BF16)
            v2 = jnp.concatenate([vp, v_ref[rows, :]], axis=0).astype(BF16)
            qm = jnp.concatenate([jnp.where(lo, q2, 0.0), jnp.where(lo, 0.0, q2)], axis=0).astype(BF16)
            s = lax.dot_general(qm, k2, _NT, preferred_element_type=F32)
            s = s + bias_ref[pi, variant]
            m = jnp.max(s, axis=-1, keepdims=True)
            p = jnp.exp2(s - m)
            l = jnp.sum(p, axis=-1, keepdims=True)
            pv = jnp.dot(p.astype(BF16), v2, preferred_element_type=F32)
            m_b = jnp.where(lo, m[:L], m[L:])
            l_b = jnp.where(lo, l[:L], l[L:])
            acc_b = jnp.where(lo, pv[:L], pv[L:])
            if pi > 0:
                m_old = m_sc[rows, :]
                m_new = jnp.maximum(m_old, m_b)
                w_old = jnp.exp2(m_old - m_new)
                w_b = jnp.exp2(m_b - m_new)
                l_b = w_old * l_sc[rows, :] + w_b * l_b
                acc_b = w_old * acc_sc[rows, :] + w_b * acc_b
                m_b = m_new
            if pi < last:
                m_sc[rows, :] = m_b
                l_sc[rows, :] = l_b
                acc_sc[rows, :] = acc_b
            else:
                o_ref[rows, :] = acc_b / l_b
            return carry

        lax.fori_loop(0, span // L, unit, 0, unroll=unroll)


def _dilated(proj, bias, patterns=A_PATTERNS, unroll=8):
    bsz, t, _ = proj.shape
    L = A_BLOCK
    span = L * max(d for _, d in patterns)
    assert t % span == 0
    npair = A_WIDTH // LANES
    qc, kc, vc = A_COL0 // LANES, A_COL0 // LANES + npair, A_COL0 // LANES + 2 * npair

    def own(c):
        return pl.BlockSpec((None, span, LANES), lambda b, h, n: (b, n, c + h))

    def prev(c):
        return pl.BlockSpec((None, span, LANES), lambda b, h, n: (b, jnp.maximum(n - 1, 0), c + h))

    return pl.pallas_call(
        functools.partial(_dilated_body, span=span, patterns=patterns, unroll=unroll),
        grid=(bsz, npair, t // span),
        in_specs=[own(qc), prev(kc), own(kc), prev(vc), own(vc),
                  pl.BlockSpec((len(patterns), 2, None, 2 * L, 2 * L), lambda b, h, n: (0, 0, h, 0, 0))],
        out_specs=pl.BlockSpec((None, span, LANES), lambda b, h, n: (b, n, h)),
        out_shape=jax.ShapeDtypeStruct((bsz, t, A_WIDTH), F32),
        scratch_shapes=[pltpu.VMEM((span, LANES), F32)] * 3,
        compiler_params=pltpu.CompilerParams(
            dimension_semantics=("arbitrary", "arbitrary", "arbitrary"), vmem_limit_bytes=VMEM_LIMIT),
        name="dilated",
    )(proj, proj, proj, proj, proj,
      bias.reshape(bias.shape[:2] + (npair, 2 * L, 2 * L)))


def _cumsum_rows(tri, x):
    hi = x.astype(BF16)
    r1 = x - hi.astype(F32)
    mid = r1.astype(BF16)
    lo = (r1 - mid.astype(F32)).astype(BF16)
    return sum(jnp.dot(tri, part, preferred_element_type=F32) for part in (hi, mid, lo))


def _hgrn_gates(z, lb):
    sig = _sigmoid(z)
    fg = lb + (1.0 - lb) * sig
    return jnp.log(jnp.maximum(fg, 1e-30)) * LOG2E, (1.0 - lb) * (1.0 - sig)


def _hgrn_intra_exact(q, k, b, v, lag, sub):
    C, V = v.shape
    amat = jnp.where(lag == 0, jnp.sum(q * k, axis=-1, keepdims=True), 0.0)
    for d in range(1, sub):
        ks = pltpu.roll(k, d, 0)
        bs = pltpu.roll(b, d, 0)
        pr = q * ks * jnp.exp2(jnp.minimum(b - bs, 0.0))
        amat = jnp.where(lag == d, jnp.sum(pr, axis=-1, keepdims=True), amat)
    o = jnp.dot(amat.astype(BF16), v.astype(BF16), preferred_element_type=F32)
    for j in range(C // sub - 1):
        e = sub * (j + 1)
        r = b[e - 1:e]
        kh = (k[e - sub:e] * jnp.exp2(r - b[e - sub:e])).astype(BF16)
        qt = (q[e:] * jnp.exp2(b[e:] - r)).astype(BF16)
        a = lax.dot_general(qt, kh, _NT, preferred_element_type=F32)
        contrib = jnp.dot(a.astype(BF16), v[e - sub:e].astype(BF16), preferred_element_type=F32)
        o = o + jnp.concatenate([jnp.zeros((e, V), F32), contrib], axis=0)
    return o


def _hgrn_scores_factored(q, k, b, causal, sub):
    C, K = q.shape
    qs, ks = [], []
    for j in range(C // sub):
        s0 = sub * j
        r = b[s0 - 1:s0] if j > 0 else jnp.zeros((1, K), F32)
        qt = (q[s0:] * jnp.exp2(b[s0:] - r)).astype(BF16)
        kh = (k[s0:s0 + sub] * jnp.exp2(r - b[s0:s0 + sub])).astype(BF16)
        above = [jnp.zeros((s0, K), BF16)] if s0 > 0 else []
        below = [jnp.zeros((C - s0 - sub, K), BF16)] if C - s0 - sub > 0 else []
        qs.append(jnp.concatenate(above + [qt], axis=0))
        ks.append(jnp.concatenate(above + [kh] + below, axis=0))
    a = lax.dot_general(jnp.concatenate(qs, axis=1), jnp.concatenate(ks, axis=1), _NT,
                        preferred_element_type=F32)
    return jnp.where(causal, a, 0.0)


def _hgrn_body(q_ref, f_ref, i_ref, g_ref, lb_ref, gn_ref, o_ref, st_ref, *, chunk, sub, fsub, nchunk):
    C = chunk
    K, V = B_KEY_DIM, B_VAL_DIM

    @pl.when(pl.program_id(1) == 0)
    def _():
        st_ref[...] = jnp.zeros_like(st_ref)

    ri = lax.broadcasted_iota(jnp.int32, (C, C), 0)
    ci = lax.broadcasted_iota(jnp.int32, (C, C), 1)
    causal = ri >= ci
    tri = causal.astype(F32).astype(BF16)
    lag = jnp.where((ri // sub == ci // sub) & causal, ri - ci, -1)

    def run(factored):
        def chunk_body(ic, carry):
            r0 = pl.multiple_of(ic * C, C)
            rows = pl.ds(r0, C)
            outs = []
            g2_all, k_all = _hgrn_gates(f_ref[0, rows, :], lb_ref[...])
            b_all = _cumsum_rows(tri, g2_all)
            for h in range(B_HEADS):
                kc = slice(h * K, (h + 1) * K)
                vc = slice(h * V, (h + 1) * V)
                qraw = q_ref[0, rows, kc]
                v = i_ref[0, rows, vc]
                gate = g_ref[0, rows, vc]
                k, b = k_all[:, kc], b_all[:, kc]
                q = qraw * _sigmoid(qraw)
                st = st_ref[h]
                vt = v.T.astype(BF16)
                qe = (q * jnp.exp2(b)).astype(BF16)
                o = lax.dot_general(qe, st.astype(BF16), _NT, preferred_element_type=F32)
                if factored:
                    a = _hgrn_scores_factored(q, k, b, causal, fsub).astype(BF16)
                    o = o + jnp.dot(a, v.astype(BF16), preferred_element_type=F32)
                else:
                    o = o + _hgrn_intra_exact(q, k, b, v, lag, sub)
                bl = b[C - 1:C]
                kd = (k * jnp.exp2(bl - b)).astype(BF16)
                st_ref[h] = st * jnp.exp2(bl) + jnp.dot(vt, kd, preferred_element_type=F32)
                on = _rms(o, gn_ref[...])
                outs.append(on * (gate * _sigmoid(gate)))
            o_ref[0, rows, :] = jnp.concatenate(outs, axis=-1)
            return carry

        lax.fori_loop(0, nchunk, chunk_body, 0, unroll=4 if factored else 1)

    g2_all, _ = _hgrn_gates(f_ref[0], lb_ref[...])
    tb = g2_all.shape[0]
    worst = jnp.max(-jnp.sum(g2_all.reshape(tb // fsub, fsub, g2_all.shape[1]), axis=1))
    lax.cond(worst < HGRN_FACTOR_LIMIT, lambda: run(True), lambda: run(False))


def _hgrn(proj, lb, gnorm, tb=256):
    bsz, t, _ = proj.shape
    C = HGRN_CHUNK
    assert t % tb == 0 and tb % C == 0
    kw, vw = B_QK_WIDTH, B_V_WIDTH
    return pl.pallas_call(
        functools.partial(_hgrn_body, chunk=C, sub=HGRN_SUB, fsub=HGRN_FSUB, nchunk=tb // C),
        grid=(bsz, t // tb),
        in_specs=[
            pl.BlockSpec((1, tb, kw), lambda b, i: (b, i, 0)),
            pl.BlockSpec((1, tb, kw), lambda b, i: (b, i, 1)),
            pl.BlockSpec((1, tb, vw), lambda b, i: (b, i, 2 * kw // vw)),
            pl.BlockSpec((1, tb, vw), lambda b, i: (b, i, 2 * kw // vw + 1)),
            pl.BlockSpec((1, kw), lambda b, i: (0, 0)),
            pl.BlockSpec((1, B_VAL_DIM), lambda b, i: (0, 0)),
        ],
        out_specs=pl.BlockSpec((1, tb, vw), lambda b, i: (b, i, 0)),
        out_shape=jax.ShapeDtypeStruct((bsz, t, vw), F32),
        scratch_shapes=[pltpu.VMEM((B_HEADS, B_VAL_DIM, B_KEY_DIM), F32)],
        compiler_params=pltpu.CompilerParams(
            dimension_semantics=("arbitrary", "arbitrary"), vmem_limit_bytes=VMEM_LIMIT),
        name="hgrn2",
    )(proj, proj, proj, proj, lb, gnorm)


def _moba_select_body(q_ref, k_ref, qx_ref, kmean, *, nb, topk, qchunk):
    L = C_BLOCK
    E = HEAD_DIM
    G = LANES // 4
    kmean[...] = jnp.zeros_like(kmean)
    for n in range(nb):
        kmean[n:n + 1, :] = jnp.mean(k_ref[0, n * L:(n + 1) * L, :], axis=0, keepdims=True)
    lane = lax.broadcasted_iota(jnp.int32, (qchunk, LANES), 1)
    grp = lane // G
    blk_i = lax.broadcasted_iota(jnp.int32, (G, qchunk), 0)
    blk_f = blk_i.astype(F32)
    qcol = lax.broadcasted_iota(jnp.int32, (G, qchunk), 1)
    place_r = lax.broadcasted_iota(jnp.int32, (G, LANES), 0)
    place_c = lax.broadcasted_iota(jnp.int32, (G, LANES), 1)
    qscale = (E ** -0.5) * LOG2E

    def chunk_body(c, carry):
        r0 = pl.multiple_of(c * qchunk, qchunk)
        q2 = q_ref[0, pl.ds(r0, qchunk), :]
        qblk = (qcol + r0) // L
        for half in range(2):
            hmask = (lane < E) if half == 0 else (lane >= E)
            qf = jnp.where(hmask, q2, 0.0)
            gate = lax.dot_general(kmean[...], qf, _NT, preferred_element_type=F32,
                                   precision=lax.Precision.HIGHEST)
            gate = jnp.where(blk_i < qblk, gate, NEG_BIG)
            picked = jnp.zeros((G, qchunk), jnp.bool_)
            for _ in range(topk):
                mx = jnp.max(gate, axis=0, keepdims=True)
                ix = jnp.min(jnp.where(gate == mx, blk_f, float(G)), axis=0, keepdims=True)
                hit = blk_f == ix
                picked = picked | (hit & (mx > 0.5 * NEG_BIG))
                gate = jnp.where(hit, -jnp.inf, gate)
            wanted = picked | (blk_i == qblk)
            sel_grp = 2 if half == 0 else 0
            place = (place_c == place_r + sel_grp * G).astype(BF16)
            flags = jnp.dot(wanted.astype(F32).T.astype(BF16), place, preferred_element_type=F32)
            ext = jnp.where((grp == sel_grp + 1) | ((grp == sel_grp) & (flags < 0.5)), NEG_BIG, 0.0)
            qx_ref[0, half, pl.ds(r0, qchunk), :] = jnp.where(hmask, qf * qscale, ext).astype(BF16)
        return carry

    lax.fori_loop(0, q_ref.shape[1] // qchunk, chunk_body, 0)


def _moba_body(qx, k_ref, v_ref, bias_ref, o_ref, kbh, vtb, sbuf, pbuf, acc,
               *, nb, ntile, unroll):
    L = C_BLOCK
    E = HEAD_DIM
    G = LANES // 4
    U = unroll
    assert U >= 2
    qb = pl.program_id(2)
    lane = lax.broadcasted_iota(jnp.int32, (L, LANES), 1)

    @pl.when(qb == 0)
    def _():
        ones_rows = jnp.ones((VT_ROWS - E, L), BF16)
        for n in range(nb):
            blk = k_ref[0, n * L:(n + 1) * L, :]
            kbh[n, 0] = jnp.where(lane < E, blk, (lane == 2 * G + n).astype(F32)).astype(BF16)
            kbh[n, 1] = jnp.where(lane >= E, blk, (lane == n).astype(F32)).astype(BF16)
            vt = v_ref[0, n * L:(n + 1) * L, :].T.astype(BF16)
            for half in range(2):
                vtb[n, half, 0:E, :] = vt[half * E:(half + 1) * E, :]
                vtb[n, half, E:, :] = ones_rows
        kbh[nb, 0] = (lane == 3 * G).astype(BF16)
        kbh[nb, 1] = (lane == G).astype(BF16)

    def scores_stage(dist, slot_):
        kb = jnp.where(dist <= qb, qb - dist, nb)
        tile = jnp.minimum(dist, ntile - 1)
        for half in range(2):
            s = lax.dot_general(kbh[kb, half], qx[0, half], _NT, preferred_element_type=F32)
            sbuf[slot_, half] = s + bias_ref[half, tile]

    def softmax_stage(slot_, m_old):
        out = []
        for half in range(2):
            s = sbuf[slot_, half]
            m_new = jnp.maximum(m_old[half], jnp.max(s, axis=0, keepdims=True))
            pbuf[slot_, half] = jnp.exp2(s - m_new).astype(BF16)
            out.append(m_new)
        return tuple(out)

    def pv_stage(dist, slot_, alpha):
        kb = jnp.clip(qb - dist, 0, nb - 1)
        for half in range(2):
            acc[half] = alpha[half] * acc[half] + jnp.dot(vtb[kb, half], pbuf[slot_, half],
                                                          preferred_element_type=F32)

    pbuf[...] = jnp.zeros_like(pbuf)
    acc[...] = jnp.zeros_like(acc)
    scores_stage(0, 0)

    def trip(j, carry):
        m_old, alpha = carry[0:2], carry[2:4]
        for u in range(U):
            dist = j * U + u
            pv_stage(dist - 1, (u - 1) % U, alpha)
            m_new = softmax_stage(u, m_old)
            scores_stage(dist + 1, (u + 1) % U)
            alpha = tuple(jnp.exp2(a - b) for a, b in zip(m_old, m_new))
            m_old = m_new
        return m_old + alpha

    row = jnp.full((1, L), NEG_BIG, F32)
    one = jnp.ones((1, L), F32)
    ntrips = (qb + U) // U
    carry = lax.fori_loop(0, ntrips, trip, (row, row, one, one))
    pv_stage(ntrips * U - 1, (U - 1) % U, carry[2:4])
    out_t = jnp.concatenate([acc[half, 0:E, :] / acc[half, E:E + 1, :] for half in range(2)], axis=0)
    o_ref[0] = out_t.T


def _moba(proj, bias_tiles, unroll=2):
    bsz, t, _ = proj.shape
    L = C_BLOCK
    assert t % L == 0
    nb = t // L
    assert nb <= LANES // 4, "one selection lane per key block and lane group"
    ntile = bias_tiles.shape[1]
    qcol, kcol, vcol = (C_COL0 // LANES, (C_COL0 + C_WIDTH) // LANES, (C_COL0 + 2 * C_WIDTH) // LANES)
    npair = C_HEADS // 2
    qchunk = min(2048, t)
    assert t % qchunk == 0
    qx = pl.pallas_call(
        functools.partial(_moba_select_body, nb=nb, topk=min(C_TOPK, nb), qchunk=qchunk),
        grid=(bsz, npair),
        in_specs=[
            pl.BlockSpec((1, t, LANES), lambda b, h: (b, 0, qcol + h)),
            pl.BlockSpec((1, t, LANES), lambda b, h: (b, 0, kcol + h)),
        ],
        out_specs=pl.BlockSpec((1, 2, t, LANES), lambda b, h: (b, h, 0, 0)),
        out_shape=jax.ShapeDtypeStruct((bsz, C_HEADS, t, LANES), BF16),
        scratch_shapes=[pltpu.VMEM((LANES // 4, LANES), F32)],
        compiler_params=pltpu.CompilerParams(
            dimension_semantics=("arbitrary", "arbitrary"), vmem_limit_bytes=VMEM_LIMIT),
        name="moba_select",
    )(proj, proj)
    return pl.pallas_call(
        functools.partial(_moba_body, nb=nb, ntile=ntile, unroll=unroll),
        grid=(bsz, npair, nb),
        in_specs=[
            pl.BlockSpec((1, 2, L, LANES), lambda b, h, i: (b, h, i, 0)),
            pl.BlockSpec((1, t, LANES), lambda b, h, i: (b, 0, kcol + h)),
            pl.BlockSpec((1, t, LANES), lambda b, h, i: (b, 0, vcol + h)),
            pl.BlockSpec((2, ntile, L, L), lambda b, h, i: (h, 0, 0, 0)),
        ],
        out_specs=pl.BlockSpec((1, L, LANES), lambda b, h, i: (b, i, h)),
        out_shape=jax.ShapeDtypeStruct((bsz, t, C_WIDTH), F32),
        scratch_shapes=[
            pltpu.VMEM((nb + 1, 2, L, LANES), BF16),
            pltpu.VMEM((nb, 2, VT_ROWS, L), BF16),
            pltpu.VMEM((unroll, 2, L, L), F32),
            pltpu.VMEM((unroll, 2, L, L), BF16),
            pltpu.VMEM((2, VT_ROWS, L), F32),
        ],
        compiler_params=pltpu.CompilerParams(
            dimension_semantics=("arbitrary", "arbitrary", "arbitrary"),
            vmem_limit_bytes=VMEM_LIMIT),
        name="moba",
    )(qx, proj, proj, bias_tiles)


def _post_body(x_ref, ya_ref, yb_ref, yc_ref, p_ref,
               wout_ref, nffn_ref, wup_ref, cw_ref, cb_ref, wdown_ref, nple_ref, wpg_ref, wpe_ref,
               nfin_ref, out_ref, carry_ref, *, final, fch):
    tm = x_ref.shape[1]

    @pl.when(pl.program_id(1) == 0)
    def _():
        carry_ref[...] = jnp.zeros_like(carry_ref)

    x = x_ref[0]
    x = x + jnp.dot(ya_ref[0].astype(BF16), wout_ref[0:A_WIDTH, :], preferred_element_type=F32)
    x = x + jnp.dot(yb_ref[0].astype(BF16), wout_ref[A_WIDTH:A_WIDTH + B_V_WIDTH, :],
                    preferred_element_type=F32)
    x = x + jnp.dot(yc_ref[0].astype(BF16), wout_ref[A_WIDTH + B_V_WIDTH:, :],
                    preferred_element_type=F32)

    h = _rms(x, nffn_ref[...]).astype(BF16)
    row = lax.broadcasted_iota(jnp.int32, (tm, 1), 0)

    def conv_cols(c0):
        cols = slice(c0, c0 + fch)
        u = jnp.dot(h, wup_ref[:, cols], preferred_element_type=F32)
        prev = carry_ref[:, cols]
        carry_ref[:, cols] = u[tm - 8:tm]
        u1 = jnp.where(row == 0, prev[7:8], pltpu.roll(u, 1, 0))
        u2 = jnp.where(row == 0, prev[6:7], jnp.where(row == 1, prev[7:8], pltpu.roll(u, 2, 0)))
        w = cw_ref[:, cols]
        return w[0:1] * u2 + w[1:2] * u1 + w[2:3] * u + cb_ref[:, cols]

    acc = jnp.zeros(x.shape, F32)
    for j in range(D_FF // fch):
        ug = conv_cols(j * fch)
        uv = conv_cols(D_FF + j * fch)
        act = (ug * _sigmoid(ug)) * uv
        acc = acc + jnp.dot(act.astype(BF16), wdown_ref[j * fch:(j + 1) * fch, :],
                            preferred_element_type=F32)
    x = x + acc

    hp = _rms(x, nple_ref[...]).astype(BF16)
    gate = _sigmoid(jnp.dot(hp, wpg_ref[...], preferred_element_type=F32))
    x = x + gate * jnp.dot(p_ref[0].astype(BF16), wpe_ref[...], preferred_element_type=F32)
    if final:
        x = _rms(x, nfin_ref[...])
    out_ref[0] = x


def _post(x, ya, yb, yc, p, wout, nffn, wup, cw, cb, wdown, nple, wpg, wpe, nfin, final,
          tm=512, fch=256):
    bsz, t, d = x.shape
    assert t % tm == 0 and D_FF % fch == 0

    def tok(width):
        return pl.BlockSpec((1, tm, width), lambda b, i: (b, i, 0))

    def const(shape):
        return pl.BlockSpec(shape, lambda b, i: (0,) * len(shape), pipeline_mode=pl.Buffered(1))

    return pl.pallas_call(
        functools.partial(_post_body, final=final, fch=fch),
        grid=(bsz, t // tm),
        in_specs=[tok(d), tok(A_WIDTH), tok(B_V_WIDTH), tok(C_WIDTH), tok(PLE_DIM),
                  const(wout.shape), const(nffn.shape), const(wup.shape), const(cw.shape),
                  const(cb.shape), const(wdown.shape), const(nple.shape), const(wpg.shape),
                  const(wpe.shape), const(nfin.shape)],
        out_specs=tok(d),
        out_shape=jax.ShapeDtypeStruct((bsz, t, d), F32),
        scratch_shapes=[pltpu.VMEM((8, 2 * D_FF), F32)],
        compiler_params=pltpu.CompilerParams(
            dimension_semantics=("arbitrary", "arbitrary"), vmem_limit_bytes=VMEM_LIMIT),
        name="post",
    )(x, ya, yb, yc, p, wout, nffn, wup, cw, cb, wdown, nple, wpg, wpe, nfin)


def _dilated_bias(bias_a, window, dilation):
    L = A_BLOCK
    m = jnp.arange(3 * L - 1)
    u = bias_a[:, _rel_bucket((2 * L - 1 - m) * dilation)]
    tile = _toeplitz(u, L, 2 * L) * LOG2E
    qi = jnp.arange(L)[:, None]
    kj = jnp.arange(2 * L)[None, :]
    delta = qi + L - kj
    band = (delta >= 0) & (delta <= window // dilation)
    return jnp.stack([jnp.where(band, tile, NEG_BIG), jnp.where(band & (kj >= L), tile, NEG_BIG)])


def _moba_bias(bias_c, t):
    L = C_BLOCK
    nb = t // L
    dist = np.arange(0, t)
    bk = _rel_bucket_np(dist)
    nfar = nb
    for dblk in range(nb, 0, -1):
        if np.all(bk[max(dblk * L - (L - 1), 0):] == bk[-1]):
            nfar = dblk
    m = jnp.arange(2 * L - 1)
    tiles = []
    for dblk in range(nfar):
        u = bias_c[:, _rel_bucket(dblk * L + m - (L - 1))]
        tiles.append(_toeplitz(u, L, L))
    far = jnp.broadcast_to(bias_c[:, int(bk[-1])][:, None, None], (bias_c.shape[0], L, L))
    tiles.append(far)
    kj = jnp.arange(L)[:, None]
    qi = jnp.arange(L)[None, :]
    tiles[0] = jnp.where(qi >= kj, tiles[0] * LOG2E, NEG_BIG)
    tiles[1:] = [tl * LOG2E for tl in tiles[1:]]
    return jnp.stack(tiles, axis=1)


def kernel(x, p, rel_bias, norm_mix, w_in, w_out, lower_bounds, hgrn_gnorm, norm_ffn, w_up, conv_w,
           conv_b, w_down, norm_ple, w_pe, w_pg, norm_final):
    bsz, t, d = x.shape
    depth = w_in.shape[0]
    lb_sm = jax.nn.softmax(lower_bounds.astype(F32), axis=0)
    lb_all = jnp.cumsum(lb_sm, axis=0) - lb_sm[0]
    bias_a = rel_bias[:A_HEADS].astype(F32)
    bias_c = rel_bias[A_HEADS:].astype(F32)
    bias_a_tiles = jnp.stack([_dilated_bias(bias_a, win, dil) for win, dil in A_PATTERNS])
    bias_c_tiles = _moba_bias(bias_c, t)

    a_end = 3 * A_WIDTH
    b_end = a_end + B_GROUP_WIDTH
    for i in range(depth):
        w = w_in[i]
        w_perm = jnp.concatenate([w[:, a_end:b_end], w[:, :a_end], w[:, b_end:]], axis=1).astype(BF16)
        proj = _inproj(x.reshape(bsz * t, d), norm_mix[i][None], w_perm).reshape(bsz, t, IN_WIDTH)
        ya = _dilated(proj, bias_a_tiles)
        yb = _hgrn(proj, lb_all[i][None], hgrn_gnorm[i].astype(F32)[None])
        yc = _moba(proj, bias_c_tiles)
        x = _post(x, ya, yb, yc, p[i], w_out[i].astype(BF16), norm_ffn[i][None],
                  w_up[i].astype(BF16), conv_w[i], conv_b[i][None], w_down[i].astype(BF16),
                  norm_ple[i][None], w_pg[i].astype(BF16), w_pe[i].astype(BF16), norm_final[None],
                  final=(i == depth - 1))
    return x
```

```python
import functools
import math

import numpy as np
import jax
import jax.numpy as jnp
from jax import lax
from jax.experimental import pallas as pl
from jax.experimental.pallas import tpu as pltpu

F32 = jnp.float32
BF16 = jnp.bfloat16

HEAD_DIM = 64
A_HEADS = 6
A_PATTERNS = ((2048, 16), (512, 4), (128, 1))
A_BLOCK = 128
B_HEADS = 4
B_KEY_DIM = 128
B_VAL_DIM = 64
C_HEADS = 6
C_BLOCK = 256
C_TOPK = 3
N_BUCKETS = 32
MAX_DISTANCE = 2048
D_FF = 2816
PLE_DIM = 256
EPS = 1e-6
NEG_BIG = -1e30

A_WIDTH = A_HEADS * HEAD_DIM
B_QK_WIDTH = B_HEADS * B_KEY_DIM
B_V_WIDTH = B_HEADS * B_VAL_DIM
C_WIDTH = C_HEADS * HEAD_DIM
IN_WIDTH = 3 * A_WIDTH + 2 * B_QK_WIDTH + 2 * B_V_WIDTH + 3 * C_WIDTH
LANES = 128

B_GROUP_WIDTH = 2 * B_QK_WIDTH + 2 * B_V_WIDTH
A_COL0 = B_GROUP_WIDTH
C_COL0 = A_COL0 + 3 * A_WIDTH

LOG2E = 1.4426950408889634
VT_ROWS = HEAD_DIM + 16

HGRN_CHUNK = 64
HGRN_SUB = 8
HGRN_FSUB = 16
HGRN_FACTOR_LIMIT = 64.0
VMEM_LIMIT = 56 * 1024 * 1024

_NT = (((1,), (1,)), ((), ()))


def _rel_bucket(dist):
    max_exact = N_BUCKETS // 2
    d = jnp.maximum(dist, 0)
    df = jnp.maximum(d, max_exact).astype(F32)
    large = max_exact + (jnp.log(df / max_exact) / math.log(MAX_DISTANCE / max_exact)
                         * (N_BUCKETS - max_exact)).astype(jnp.int32)
    large = jnp.minimum(large, N_BUCKETS - 1)
    return jnp.where(d < max_exact, d, large)


def _rel_bucket_np(dist):
    max_exact = N_BUCKETS // 2
    d = np.maximum(dist, 0)
    df = np.maximum(d, max_exact).astype(np.float64)
    large = max_exact + (np.log(df / max_exact) / math.log(MAX_DISTANCE / max_exact)
                         * (N_BUCKETS - max_exact)).astype(np.int64)
    large = np.minimum(large, N_BUCKETS - 1)
    return np.where(d < max_exact, d, large)


def _toeplitz(u, rows, cols):
    n = rows + cols - 1
    w = jnp.concatenate([u, jnp.zeros(u.shape[:-1] + (1,), u.dtype)], axis=-1)
    flat = jnp.tile(w, rows)[..., : rows * n]
    m = flat.reshape(u.shape[:-1] + (rows, n))
    return m[..., rows - 1: rows - 1 + cols]


def _rms(x, g):
    return x * lax.rsqrt(jnp.mean(x * x, axis=-1, keepdims=True) + EPS) * g


def _sigmoid(x):
    return 1.0 / (1.0 + jnp.exp(-x))


def _inproj_body(x_ref, g_ref, w_ref, o_ref, *, nch):
    h = _rms(x_ref[...], g_ref[...]).astype(BF16)
    for j in range(0, w_ref.shape[1], nch):
        o_ref[:, j:j + nch] = jnp.dot(h, w_ref[:, j:j + nch], preferred_element_type=F32)


def _inproj(x2, gain, w_bf, tm=512, nch=768):
    n, d = x2.shape
    width = w_bf.shape[1]
    return pl.pallas_call(
        functools.partial(_inproj_body, nch=nch),
        grid=(n // tm,),
        in_specs=[
            pl.BlockSpec((tm, d), lambda i: (i, 0)),
            pl.BlockSpec((1, d), lambda i: (0, 0)),
            pl.BlockSpec((d, width), lambda i: (0, 0), pipeline_mode=pl.Buffered(1)),
        ],
        out_specs=pl.BlockSpec((tm, width), lambda i: (i, 0)),
        out_shape=jax.ShapeDtypeStruct((n, width), F32),
        compiler_params=pltpu.CompilerParams(
            dimension_semantics=("arbitrary",), vmem_limit_bytes=VMEM_LIMIT),
        name="inproj",
    )(x2, gain, w_bf)


def _dilated_body(q_ref, kp_ref, k_ref, vp_ref, v_ref, bias_ref, o_ref, m_sc, l_sc, acc_sc,
                  *, span, patterns, unroll):
    L = A_BLOCK
    first_span = pl.program_id(2) == 0
    lo = lax.broadcasted_iota(jnp.int32, (L, LANES), 1) < HEAD_DIM
    scale = (HEAD_DIM ** -0.5) * LOG2E
    last = len(patterns) - 1

    for pi, (window, d) in enumerate(patterns):
        nblk = span // (L * d)

        def rows_at(start, d=d):
            return pl.ds(start, L) if d == 1 else pl.ds(start, L, stride=d)

        def unit(u, carry, pi=pi, d=d, nblk=nblk, rows_at=rows_at):
            r = u % d
            n = u // d
            rows = rows_at(r + d * L * n)
            rows_prev_span = rows_at(r + d * L * (nblk - 1))
            if nblk == 1:
                kp, vp = kp_ref[rows_prev_span, :], vp_ref[rows_prev_span, :]
                variant = first_span.astype(jnp.int32)
            else:
                rows_prev_own = rows_at(r + d * L * jnp.maximum(n - 1, 0))
                from_prev_span = n == 0
                kp = jnp.where(from_prev_span, kp_ref[rows_prev_span, :], k_ref[rows_prev_own, :])
                vp = jnp.where(from_prev_span, vp_ref[rows_prev_span, :], v_ref[rows_prev_own, :])
                variant = (from_prev_span & first_span).astype(jnp.int32)
            q2 = q_ref[rows, :] * scale
            k2 = jnp.concatenate([kp, k_ref[rows, :]], axis=0).astype(BF16)
            v2 = jnp.concatenate([vp, v_ref[rows, :]], axis=0).astype(BF16)
            qm = jnp.concatenate([jnp.where(lo, q2, 0.0), jnp.where(lo, 0.0, q2)], axis=0).astype(BF16)
            s = lax.dot_general(qm, k2, _NT, preferred_element_type=F32)
            s = s + bias_ref[pi, variant]
            m = jnp.max(s, axis=-1, keepdims=True)
            p = jnp.exp2(s - m)
            l = jnp.sum(p, axis=-1, keepdims=True)
            pv = jnp.dot(p.astype(BF16), v2, preferred_element_type=F32)
            m_b = jnp.where(lo, m[:L], m[L:])
            l_b = jnp.where(lo, l[:L], l[L:])
            acc_b = jnp.where(lo, pv[:L], pv[L:])
            if pi > 0:
                m_old = m_sc[rows, :]
                m_new = jnp.maximum(m_old, m_b)
                w_old = jnp.exp2(m_old - m_new)
                w_b = jnp.exp2(m_b - m_new)
                l_b = w_old * l_sc[rows, :] + w_b * l_b
                acc_b = w_old * acc_sc[rows, :] + w_b * acc_b
                m_b = m_new
            if pi < last:
                m_sc[rows, :] = m_b
                l_sc[rows, :] = l_b
                acc_sc[rows, :] = acc_b
            else:
                o_ref[rows, :] = acc_b / l_b
            return carry

        lax.fori_loop(0, span // L, unit, 0, unroll=unroll)


def _dilated(proj, bias, patterns=A_PATTERNS, unroll=8):
    bsz, t, _ = proj.shape
    L = A_BLOCK
    span = L * max(d for _, d in patterns)
    assert t % span == 0
    npair = A_WIDTH // LANES
    qc, kc, vc = A_COL0 // LANES, A_COL0 // LANES + npair, A_COL0 // LANES + 2 * npair

    def own(c):
        return pl.BlockSpec((None, span, LANES), lambda b, h, n: (b, n, c + h))

    def prev(c):
        return pl.BlockSpec((None, span, LANES), lambda b, h, n: (b, jnp.maximum(n - 1, 0), c + h))

    return pl.pallas_call(
        functools.partial(_dilated_body, span=span, patterns=patterns, unroll=unroll),
        grid=(bsz, npair, t // span),
        in_specs=[own(qc), prev(kc), own(kc), prev(vc), own(vc),
                  pl.BlockSpec((len(patterns), 2, None, 2 * L, 2 * L), lambda b, h, n: (0, 0, h, 0, 0))],
        out_specs=pl.BlockSpec((None, span, LANES), lambda b, h, n: (b, n, h)),
        out_shape=jax.ShapeDtypeStruct((bsz, t, A_WIDTH), F32),
        scratch_shapes=[pltpu.VMEM((span, LANES), F32)] * 3,
        compiler_params=pltpu.CompilerParams(
            dimension_semantics=("arbitrary", "arbitrary", "arbitrary"), vmem_limit_bytes=VMEM_LIMIT),
        name="dilated",
    )(proj, proj, proj, proj, proj,
      bias.reshape(bias.shape[:2] + (npair, 2 * L, 2 * L)))


def _cumsum_rows(tri, x):
    hi = x.astype(BF16)
    r1 = x - hi.astype(F32)
    mid = r1.astype(BF16)
    lo = (r1 - mid.astype(F32)).astype(BF16)
    return sum(jnp.dot(tri, part, preferred_element_type=F32) for part in (hi, mid, lo))


def _hgrn_gates(z, lb):
    sig = _sigmoid(z)
    fg = lb + (1.0 - lb) * sig
    return jnp.log(jnp.maximum(fg, 1e-30)) * LOG2E, (1.0 - lb) * (1.0 - sig)


def _hgrn_intra_exact(q, k, b, v, lag, sub):
    C, V = v.shape
    amat = jnp.where(lag == 0, jnp.sum(q * k, axis=-1, keepdims=True), 0.0)
    for d in range(1, sub):
        ks = pltpu.roll(k, d, 0)
        bs = pltpu.roll(b, d, 0)
        pr = q * ks * jnp.exp2(jnp.minimum(b - bs, 0.0))
        amat = jnp.where(lag == d, jnp.sum(pr, axis=-1, keepdims=True), amat)
    o = jnp.dot(amat.astype(BF16), v.astype(BF16), preferred_element_type=F32)
    for j in range(C // sub - 1):
        e = sub * (j + 1)
        r = b[e - 1:e]
        kh = (k[e - sub:e] * jnp.exp2(r - b[e - sub:e])).astype(BF16)
        qt = (q[e:] * jnp.exp2(b[e:] - r)).astype(BF16)
        a = lax.dot_general(qt, kh, _NT, preferred_element_type=F32)
        contrib = jnp.dot(a.astype(BF16), v[e - sub:e].astype(BF16), preferred_element_type=F32)
        o = o + jnp.concatenate([jnp.zeros((e, V), F32), contrib], axis=0)
    return o


def _hgrn_scores_factored(q, k, b, causal, sub):
    C, K = q.shape
    qs, ks = [], []
    for j in range(C // sub):
        s0 = sub * j
        r = b[s0 - 1:s0] if j > 0 else jnp.zeros((1, K), F32)
        qt = (q[s0:] * jnp.exp2(b[s0:] - r)).astype(BF16)
        kh = (k[s0:s0 + sub] * jnp.exp2(r - b[s0:s0 + sub])).astype(BF16)
        above = [jnp.zeros((s0, K), BF16)] if s0 > 0 else []
        below = [jnp.zeros((C - s0 - sub, K), BF16)] if C - s0 - sub > 0 else []
        qs.append(jnp.concatenate(above + [qt], axis=0))
        ks.append(jnp.concatenate(above + [kh] + below, axis=0))
    a = lax.dot_general(jnp.concatenate(qs, axis=1), jnp.concatenate(ks, axis=1), _NT,
                        preferred_element_type=F32)
    return jnp.where(causal, a, 0.0)


def _hgrn_body(q_ref, f_ref, i_ref, g_ref, lb_ref, gn_ref, o_ref, st_ref, *, chunk, sub, fsub, nchunk):
    C = chunk
    K, V = B_KEY_DIM, B_VAL_DIM

    @pl.when(pl.program_id(1) == 0)
    def _():
        st_ref[...] = jnp.zeros_like(st_ref)

    ri = lax.broadcasted_iota(jnp.int32, (C, C), 0)
    ci = lax.broadcasted_iota(jnp.int32, (C, C), 1)
    causal = ri >= ci
    tri = causal.astype(F32).astype(BF16)
    lag = jnp.where((ri // sub == ci // sub) & causal, ri - ci, -1)

    def run(factored):
        def chunk_body(ic, carry):
            r0 = pl.multiple_of(ic * C, C)
            rows = pl.ds(r0, C)
            outs = []
            g2_all, k_all = _hgrn_gates(f_ref[0, rows, :], lb_ref[...])
            b_all = _cumsum_rows(tri, g2_all)
            for h in range(B_HEADS):
                kc = slice(h * K, (h + 1) * K)
                vc = slice(h * V, (h + 1) * V)
                qraw = q_ref[0, rows, kc]
                v = i_ref[0, rows, vc]
                gate = g_ref[0, rows, vc]
                k, b = k_all[:, kc], b_all[:, kc]
                q = qraw * _sigmoid(qraw)
                st = st_ref[h]
                vt = v.T.astype(BF16)
                qe = (q * jnp.exp2(b)).astype(BF16)
                o = lax.dot_general(qe, st.astype(BF16), _NT, preferred_element_type=F32)
                if factored:
                    a = _hgrn_scores_factored(q, k, b, causal, fsub).astype(BF16)
                    o = o + jnp.dot(a, v.astype(BF16), preferred_element_type=F32)
                else:
                    o = o + _hgrn_intra_exact(q, k, b, v, lag, sub)
                bl = b[C - 1:C]
                kd = (k * jnp.exp2(bl - b)).astype(BF16)
                st_ref[h] = st * jnp.exp2(bl) + jnp.dot(vt, kd, preferred_element_type=F32)
                on = _rms(o, gn_ref[...])
                outs.append(on * (gate * _sigmoid(gate)))
            o_ref[0, rows, :] = jnp.concatenate(outs, axis=-1)
            return carry

        lax.fori_loop(0, nchunk, chunk_body, 0, unroll=4 if factored else 1)

    g2_all, _ = _hgrn_gates(f_ref[0], lb_ref[...])
    tb = g2_all.shape[0]
    worst = jnp.max(-jnp.sum(g2_all.reshape(tb // fsub, fsub, g2_all.shape[1]), axis=1))
    lax.cond(worst < HGRN_FACTOR_LIMIT, lambda: run(True), lambda: run(False))


def _hgrn(proj, lb, gnorm, tb=256):
    bsz, t, _ = proj.shape
    C = HGRN_CHUNK
    assert t % tb == 0 and tb % C == 0
    kw, vw = B_QK_WIDTH, B_V_WIDTH
    return pl.pallas_call(
        functools.partial(_hgrn_body, chunk=C, sub=HGRN_SUB, fsub=HGRN_FSUB, nchunk=tb // C),
        grid=(bsz, t // tb),
        in_specs=[
            pl.BlockSpec((1, tb, kw), lambda b, i: (b, i, 0)),
            pl.BlockSpec((1, tb, kw), lambda b, i: (b, i, 1)),
            pl.BlockSpec((1, tb, vw), lambda b, i: (b, i, 2 * kw // vw)),
            pl.BlockSpec((1, tb, vw), lambda b, i: (b, i, 2 * kw // vw + 1)),
            pl.BlockSpec((1, kw), lambda b, i: (0, 0)),
            pl.BlockSpec((1, B_VAL_DIM), lambda b, i: (0, 0)),
        ],
        out_specs=pl.BlockSpec((1, tb, vw), lambda b, i: (b, i, 0)),
        out_shape=jax.ShapeDtypeStruct((bsz, t, vw), F32),
        scratch_shapes=[pltpu.VMEM((B_HEADS, B_VAL_DIM, B_KEY_DIM), F32)],
        compiler_params=pltpu.CompilerParams(
            dimension_semantics=("arbitrary", "arbitrary"), vmem_limit_bytes=VMEM_LIMIT),
        name="hgrn2",
    )(proj, proj, proj, proj, lb, gnorm)


def _moba_select_body(q_ref, k_ref, qx_ref, kmean, *, nb, topk, qchunk):
    L = C_BLOCK
    E = HEAD_DIM
    G = LANES // 4
    kmean[...] = jnp.zeros_like(kmean)
    for n in range(nb):
        kmean[n:n + 1, :] = jnp.mean(k_ref[0, n * L:(n + 1) * L, :], axis=0, keepdims=True)
    lane = lax.broadcasted_iota(jnp.int32, (qchunk, LANES), 1)
    grp = lane // G
    blk_i = lax.broadcasted_iota(jnp.int32, (G, qchunk), 0)
    blk_f = blk_i.astype(F32)
    qcol = lax.broadcasted_iota(jnp.int32, (G, qchunk), 1)
    place_r = lax.broadcasted_iota(jnp.int32, (G, LANES), 0)
    place_c = lax.broadcasted_iota(jnp.int32, (G, LANES), 1)
    qscale = (E ** -0.5) * LOG2E

    def chunk_body(c, carry):
        r0 = pl.multiple_of(c * qchunk, qchunk)
        q2 = q_ref[0, pl.ds(r0, qchunk), :]
        qblk = (qcol + r0) // L
        for half in range(2):
            hmask = (lane < E) if half == 0 else (lane >= E)
            qf = jnp.where(hmask, q2, 0.0)
            gate = lax.dot_general(kmean[...], qf, _NT, preferred_element_type=F32,
                                   precision=lax.Precision.HIGHEST)
            gate = jnp.where(blk_i < qblk, gate, NEG_BIG)
            picked = jnp.zeros((G, qchunk), jnp.bool_)
            for _ in range(topk):
                mx = jnp.max(gate, axis=0, keepdims=True)
                ix = jnp.min(jnp.where(gate == mx, blk_f, float(G)), axis=0, keepdims=True)
                hit = blk_f == ix
                picked = picked | (hit & (mx > 0.5 * NEG_BIG))
                gate = jnp.where(hit, -jnp.inf, gate)
            wanted = picked | (blk_i == qblk)
            sel_grp = 2 if half == 0 else 0
            place = (place_c == place_r + sel_grp * G).astype(BF16)
            flags = jnp.dot(wanted.astype(F32).T.astype(BF16), place, preferred_element_type=F32)
            ext = jnp.where((grp == sel_grp + 1) | ((grp == sel_grp) & (flags < 0.5)), NEG_BIG, 0.0)
            qx_ref[0, half, pl.ds(r0, qchunk), :] = jnp.where(hmask, qf * qscale, ext).astype(BF16)
        return carry

    lax.fori_loop(0, q_ref.shape[1] // qchunk, chunk_body, 0)


def _moba_body(qx, k_ref, v_ref, bias_ref, o_ref, kbh, vtb, sbuf, pbuf, acc,
               *, nb, ntile, unroll):
    L = C_BLOCK
    E = HEAD_DIM
    G = LANES // 4
    U = unroll
    assert U >= 2
    qb = pl.program_id(2)
    lane = lax.broadcasted_iota(jnp.int32, (L, LANES), 1)

    @pl.when(qb == 0)
    def _():
        ones_rows = jnp.ones((VT_ROWS - E, L), BF16)
        for n in range(nb):
            blk = k_ref[0, n * L:(n + 1) * L, :]
            kbh[n, 0] = jnp.where(lane < E, blk, (lane == 2 * G + n).astype(F32)).astype(BF16)
            kbh[n, 1] = jnp.where(lane >= E, blk, (lane == n).astype(F32)).astype(BF16)
            vt = v_ref[0, n * L:(n + 1) * L, :].T.astype(BF16)
            for half in range(2):
                vtb[n, half, 0:E, :] = vt[half * E:(half + 1) * E, :]
                vtb[n, half, E:, :] = ones_rows
        kbh[nb, 0] = (lane == 3 * G).astype(BF16)
        kbh[nb, 1] = (lane == G).astype(BF16)

    def scores_stage(dist, slot_):
        kb = jnp.where(dist <= qb, qb - dist, nb)
        tile = jnp.minimum(dist, ntile - 1)
        for half in range(2):
            s = lax.dot_general(kbh[kb, half], qx[0, half], _NT, preferred_element_type=F32)
            sbuf[slot_, half] = s + bias_ref[half, tile]

    def softmax_stage(slot_, m_old):
        out = []
        for half in range(2):
            s = sbuf[slot_, half]
            m_new = jnp.maximum(m_old[half], jnp.max(s, axis=0, keepdims=True))
            pbuf[slot_, half] = jnp.exp2(s - m_new).astype(BF16)
            out.append(m_new)
        return tuple(out)

    def pv_stage(dist, slot_, alpha):
        kb = jnp.clip(qb - dist, 0, nb - 1)
        for half in range(2):
            acc[half] = alpha[half] * acc[half] + jnp.dot(vtb[kb, half], pbuf[slot_, half],
                                                          preferred_element_type=F32)

    pbuf[...] = jnp.zeros_like(pbuf)
    acc[...] = jnp.zeros_like(acc)
    scores_stage(0, 0)

    def trip(j, carry):
        m_old, alpha = carry[0:2], carry[2:4]
        for u in range(U):
            dist = j * U + u
            pv_stage(dist - 1, (u - 1) % U, alpha)
            m_new = softmax_stage(u, m_old)
            scores_stage(dist + 1, (u + 1) % U)
            alpha = tuple(jnp.exp2(a - b) for a, b in zip(m_old, m_new))
            m_old = m_new
        return m_old + alpha

    row = jnp.full((1, L), NEG_BIG, F32)
    one = jnp.ones((1, L), F32)
    ntrips = (qb + U) // U
    carry = lax.fori_loop(0, ntrips, trip, (row, row, one, one))
    pv_stage(ntrips * U - 1, (U - 1) % U, carry[2:4])
    out_t = jnp.concatenate([acc[half, 0:E, :] / acc[half, E:E + 1, :] for half in range(2)], axis=0)
    o_ref[0] = out_t.T


def _moba(proj, bias_tiles, unroll=2):
    bsz, t, _ = proj.shape
    L = C_BLOCK
    assert t % L == 0
    nb = t // L
    assert nb <= LANES // 4, "one selection lane per key block and lane group"
    ntile = bias_tiles.shape[1]
    qcol, kcol, vcol = (C_COL0 // LANES, (C_COL0 + C_WIDTH) // LANES, (C_COL0 + 2 * C_WIDTH) // LANES)
    npair = C_HEADS // 2
    qchunk = min(2048, t)
    assert t % qchunk == 0
    qx = pl.pallas_call(
        functools.partial(_moba_select_body, nb=nb, topk=min(C_TOPK, nb), qchunk=qchunk),
        grid=(bsz, npair),
        in_specs=[
            pl.BlockSpec((1, t, LANES), lambda b, h: (b, 0, qcol + h)),
            pl.BlockSpec((1, t, LANES), lambda b, h: (b, 0, kcol + h)),
        ],
        out_specs=pl.BlockSpec((1, 2, t, LANES), lambda b, h: (b, h, 0, 0)),
        out_shape=jax.ShapeDtypeStruct((bsz, C_HEADS, t, LANES), BF16),
        scratch_shapes=[pltpu.VMEM((LANES // 4, LANES), F32)],
        compiler_params=pltpu.CompilerParams(
            dimension_semantics=("arbitrary", "arbitrary"), vmem_limit_bytes=VMEM_LIMIT),
        name="moba_select",
    )(proj, proj)
    return pl.pallas_call(
        functools.partial(_moba_body, nb=nb, ntile=ntile, unroll=unroll),
        grid=(bsz, npair, nb),
        in_specs=[
            pl.BlockSpec((1, 2, L, LANES), lambda b, h, i: (b, h, i, 0)),
            pl.BlockSpec((1, t, LANES), lambda b, h, i: (b, 0, kcol + h)),
            pl.BlockSpec((1, t, LANES), lambda b, h, i: (b, 0, vcol + h)),
            pl.BlockSpec((2, ntile, L, L), lambda b, h, i: (h, 0, 0, 0)),
        ],
        out_specs=pl.BlockSpec((1, L, LANES), lambda b, h, i: (b, i, h)),
        out_shape=jax.ShapeDtypeStruct((bsz, t, C_WIDTH), F32),
        scratch_shapes=[
            pltpu.VMEM((nb + 1, 2, L, LANES), BF16),
            pltpu.VMEM((nb, 2, VT_ROWS, L), BF16),
            pltpu.VMEM((unroll, 2, L, L), F32),
            pltpu.VMEM((unroll, 2, L, L), BF16),
            pltpu.VMEM((2, VT_ROWS, L), F32),
        ],
        compiler_params=pltpu.CompilerParams(
            dimension_semantics=("arbitrary", "arbitrary", "arbitrary"),
            vmem_limit_bytes=VMEM_LIMIT),
        name="moba",
    )(qx, proj, proj, bias_tiles)


def _post_body(x_ref, ya_ref, yb_ref, yc_ref, p_ref,
               wout_ref, nffn_ref, wup_ref, cw_ref, cb_ref, wdown_ref, nple_ref, wpg_ref, wpe_ref,
               nfin_ref, out_ref, carry_ref, *, final, fch):
    tm = x_ref.shape[1]

    @pl.when(pl.program_id(1) == 0)
    def _():
        carry_ref[...] = jnp.zeros_like(carry_ref)

    x = x_ref[0]
    x = x + jnp.dot(ya_ref[0].astype(BF16), wout_ref[0:A_WIDTH, :], preferred_element_type=F32)
    x = x + jnp.dot(yb_ref[0].astype(BF16), wout_ref[A_WIDTH:A_WIDTH + B_V_WIDTH, :],
                    preferred_element_type=F32)
    x = x + jnp.dot(yc_ref[0].astype(BF16), wout_ref[A_WIDTH + B_V_WIDTH:, :],
                    preferred_element_type=F32)

    h = _rms(x, nffn_ref[...]).astype(BF16)
    row = lax.broadcasted_iota(jnp.int32, (tm, 1), 0)

    def conv_cols(c0):
        cols = slice(c0, c0 + fch)
        u = jnp.dot(h, wup_ref[:, cols], preferred_element_type=F32)
        prev = carry_ref[:, cols]
        carry_ref[:, cols] = u[tm - 8:tm]
        u1 = jnp.where(row == 0, prev[7:8], pltpu.roll(u, 1, 0))
        u2 = jnp.where(row == 0, prev[6:7], jnp.where(row == 1, prev[7:8], pltpu.roll(u, 2, 0)))
        w = cw_ref[:, cols]
        return w[0:1] * u2 + w[1:2] * u1 + w[2:3] * u + cb_ref[:, cols]

    acc = jnp.zeros(x.shape, F32)
    for j in range(D_FF // fch):
        ug = conv_cols(j * fch)
        uv = conv_cols(D_FF + j * fch)
        act = (ug * _sigmoid(ug)) * uv
        acc = acc + jnp.dot(act.astype(BF16), wdown_ref[j * fch:(j + 1) * fch, :],
                            preferred_element_type=F32)
    x = x + acc

    hp = _rms(x, nple_ref[...]).astype(BF16)
    gate = _sigmoid(jnp.dot(hp, wpg_ref[...], preferred_element_type=F32))
    x = x + gate * jnp.dot(p_ref[0].astype(BF16), wpe_ref[...], preferred_element_type=F32)
    if final:
        x = _rms(x, nfin_ref[...])
    out_ref[0] = x


def _post(x, ya, yb, yc, p, wout, nffn, wup, cw, cb, wdown, nple, wpg, wpe, nfin, final,
          tm=512, fch=256):
    bsz, t, d = x.shape
    assert t % tm == 0 and D_FF % fch == 0

    def tok(width):
        return pl.BlockSpec((1, tm, width), lambda b, i: (b, i, 0))

    def const(shape):
        return pl.BlockSpec(shape, lambda b, i: (0,) * len(shape), pipeline_mode=pl.Buffered(1))

    return pl.pallas_call(
        functools.partial(_post_body, final=final, fch=fch),
        grid=(bsz, t // tm),
        in_specs=[tok(d), tok(A_WIDTH), tok(B_V_WIDTH), tok(C_WIDTH), tok(PLE_DIM),
                  const(wout.shape), const(nffn.shape), const(wup.shape), const(cw.shape),
                  const(cb.shape), const(wdown.shape), const(nple.shape), const(wpg.shape),
                  const(wpe.shape), const(nfin.shape)],
        out_specs=tok(d),
        out_shape=jax.ShapeDtypeStruct((bsz, t, d), F32),
        scratch_shapes=[pltpu.VMEM((8, 2 * D_FF), F32)],
        compiler_params=pltpu.CompilerParams(
            dimension_semantics=("arbitrary", "arbitrary"), vmem_limit_bytes=VMEM_LIMIT),
        name="post",
    )(x, ya, yb, yc, p, wout, nffn, wup, cw, cb, wdown, nple, wpg, wpe, nfin)


def _dilated_bias(bias_a, window, dilation):
    L = A_BLOCK
    m = jnp.arange(3 * L - 1)
    u = bias_a[:, _rel_bucket((2 * L - 1 - m) * dilation)]
    tile = _toeplitz(u, L, 2 * L) * LOG2E
    qi = jnp.arange(L)[:, None]
    kj = jnp.arange(2 * L)[None, :]
    delta = qi + L - kj
    band = (delta >= 0) & (delta <= window // dilation)
    return jnp.stack([jnp.where(band, tile, NEG_BIG), jnp.where(band & (kj >= L), tile, NEG_BIG)])


def _moba_bias(bias_c, t):
    L = C_BLOCK
    nb = t // L
    dist = np.arange(0, t)
    bk = _rel_bucket_np(dist)
    nfar = nb
    for dblk in range(nb, 0, -1):
        if np.all(bk[max(dblk * L - (L - 1), 0):] == bk[-1]):
            nfar = dblk
    m = jnp.arange(2 * L - 1)
    tiles = []
    for dblk in range(nfar):
        u = bias_c[:, _rel_bucket(dblk * L + m - (L - 1))]
        tiles.append(_toeplitz(u, L, L))
    far = jnp.broadcast_to(bias_c[:, int(bk[-1])][:, None, None], (bias_c.shape[0], L, L))
    tiles.append(far)
    kj = jnp.arange(L)[:, None]
    qi = jnp.arange(L)[None, :]
    tiles[0] = jnp.where(qi >= kj, tiles[0] * LOG2E, NEG_BIG)
    tiles[1:] = [tl * LOG2E for tl in tiles[1:]]
    return jnp.stack(tiles, axis=1)


def kernel(x, p, rel_bias, norm_mix, w_in, w_out, lower_bounds, hgrn_gnorm, norm_ffn, w_up, conv_w,
           conv_b, w_down, norm_ple, w_pe, w_pg, norm_final):
    bsz, t, d = x.shape
    depth = w_in.shape[0]
    lb_sm = jax.nn.softmax(lower_bounds.astype(F32), axis=0)
    lb_all = jnp.cumsum(lb_sm, axis=0) - lb_sm[0]
    bias_a = rel_bias[:A_HEADS].astype(F32)
    bias_c = rel_bias[A_HEADS:].astype(F32)
    bias_a_tiles = jnp.stack([_dilated_bias(bias_a, win, dil) for win, dil in A_PATTERNS])
    bias_c_tiles = _moba_bias(bias_c, t)

    a_end = 3 * A_WIDTH
    b_end = a_end + B_GROUP_WIDTH
    for i in range(depth):
        w = w_in[i]
        w_perm = jnp.concatenate([w[:, a_end:b_end], w[:, :a_end], w[:, b_end:]], axis=1).astype(BF16)
        proj = _inproj(x.reshape(bsz * t, d), norm_mix[i][None], w_perm).reshape(bsz, t, IN_WIDTH)
        ya = _dilated(proj, bias_a_tiles)
        yb = _hgrn(proj, lb_all[i][None], hgrn_gnorm[i].astype(F32)[None])
        yc = _moba(proj, bias_c_tiles)
        x = _post(x, ya, yb, yc, p[i], w_out[i].astype(BF16), norm_ffn[i][None],
                  w_up[i].astype(BF16), conv_w[i], conv_b[i][None], w_down[i].astype(BF16),
                  norm_ple[i][None], w_pg[i].astype(BF16), w_pe[i].astype(BF16), norm_final[None],
                  final=(i == depth - 1))
    return x
```

```python
import functools
import math

import numpy as np
import jax
import jax.numpy as jnp
from jax import lax
from jax.experimental import pallas as pl
from jax.experimental.pallas import tpu as pltpu

F32 = jnp.float32
BF16 = jnp.bfloat16

HEAD_DIM = 64
A_HEADS = 6
A_PATTERNS = ((2048, 16), (512, 4), (128, 1))
A_BLOCK = 128
B_HEADS = 4
B_KEY_DIM = 128
B_VAL_DIM = 64
C_HEADS = 6
C_BLOCK = 256
C_TOPK = 3
N_BUCKETS = 32
MAX_DISTANCE = 2048
D_FF = 2816
PLE_DIM = 256
EPS = 1e-6
NEG_BIG = -1e30

A_WIDTH = A_HEADS * HEAD_DIM
B_QK_WIDTH = B_HEADS * B_KEY_DIM
B_V_WIDTH = B_HEADS * B_VAL_DIM
C_WIDTH = C_HEADS * HEAD_DIM
IN_WIDTH = 3 * A_WIDTH + 2 * B_QK_WIDTH + 2 * B_V_WIDTH + 3 * C_WIDTH
LANES = 128

B_GROUP_WIDTH = 2 * B_QK_WIDTH + 2 * B_V_WIDTH
A_COL0 = B_GROUP_WIDTH
C_COL0 = A_COL0 + 3 * A_WIDTH

LOG2E = 1.4426950408889634
VT_ROWS = HEAD_DIM + 16

HGRN_CHUNK = 64
HGRN_SUB = 8
HGRN_FSUB = 16
HGRN_FACTOR_LIMIT = 64.0
VMEM_LIMIT = 56 * 1024 * 1024

_NT = (((1,), (1,)), ((), ()))


def _rel_bucket(dist):
    max_exact = N_BUCKETS // 2
    d = jnp.maximum(dist, 0)
    df = jnp.maximum(d, max_exact).astype(F32)
    large = max_exact + (jnp.log(df / max_exact) / math.log(MAX_DISTANCE / max_exact)
                         * (N_BUCKETS - max_exact)).astype(jnp.int32)
    large = jnp.minimum(large, N_BUCKETS - 1)
    return jnp.where(d < max_exact, d, large)


def _rel_bucket_np(dist):
    max_exact = N_BUCKETS // 2
    d = np.maximum(dist, 0)
    df = np.maximum(d, max_exact).astype(np.float64)
    large = max_exact + (np.log(df / max_exact) / math.log(MAX_DISTANCE / max_exact)
                         * (N_BUCKETS - max_exact)).astype(np.int64)
    large = np.minimum(large, N_BUCKETS - 1)
    return np.where(d < max_exact, d, large)


def _toeplitz(u, rows, cols):
    n = rows + cols - 1
    w = jnp.concatenate([u, jnp.zeros(u.shape[:-1] + (1,), u.dtype)], axis=-1)
    flat = jnp.tile(w, rows)[..., : rows * n]
    m = flat.reshape(u.shape[:-1] + (rows, n))
    return m[..., rows - 1: rows - 1 + cols]


def _rms(x, g):
    return x * lax.rsqrt(jnp.mean(x * x, axis=-1, keepdims=True) + EPS) * g


def _sigmoid(x):
    return 1.0 / (1.0 + jnp.exp(-x))


def _inproj_body(x_ref, g_ref, w_ref, o_ref, *, nch):
    h = _rms(x_ref[...], g_ref[...]).astype(BF16)
    for j in range(0, w_ref.shape[1], nch):
        o_ref[:, j:j + nch] = jnp.dot(h, w_ref[:, j:j + nch], preferred_element_type=F32)


def _inproj(x2, gain, w_bf, tm=512, nch=768):
    n, d = x2.shape
    width = w_bf.shape[1]
    return pl.pallas_call(
        functools.partial(_inproj_body, nch=nch),
        grid=(n // tm,),
        in_specs=[
            pl.BlockSpec((tm, d), lambda i: (i, 0)),
            pl.BlockSpec((1, d), lambda i: (0, 0)),
            pl.BlockSpec((d, width), lambda i: (0, 0), pipeline_mode=pl.Buffered(1)),
        ],
        out_specs=pl.BlockSpec((tm, width), lambda i: (i, 0)),
        out_shape=jax.ShapeDtypeStruct((n, width), F32),
        compiler_params=pltpu.CompilerParams(
            dimension_semantics=("arbitrary",), vmem_limit_bytes=VMEM_LIMIT),
        name="inproj",
    )(x2, gain, w_bf)


def _dilated_body(q_ref, kp_ref, k_ref, vp_ref, v_ref, bias_ref, o_ref, m_sc, l_sc, acc_sc,
                  *, span, patterns, unroll):
    L = A_BLOCK
    first_span = pl.program_id(2) == 0
    lo = lax.broadcasted_iota(jnp.int32, (L, LANES), 1) < HEAD_DIM
    scale = (HEAD_DIM ** -0.5) * LOG2E
    last = len(patterns) - 1

    for pi, (window, d) in enumerate(patterns):
        nblk = span // (L * d)

        def rows_at(start, d=d):
            return pl.ds(start, L) if d == 1 else pl.ds(start, L, stride=d)

        def unit(u, carry, pi=pi, d=d, nblk=nblk, rows_at=rows_at):
            r = u % d
            n = u // d
            rows = rows_at(r + d * L * n)
            rows_prev_span = rows_at(r + d * L * (nblk - 1))
            if nblk == 1:
                kp, vp = kp_ref[rows_prev_span, :], vp_ref[rows_prev_span, :]
                variant = first_span.astype(jnp.int32)
            else:
                rows_prev_own = rows_at(r + d * L * jnp.maximum(n - 1, 0))
                from_prev_span = n == 0
                kp = jnp.where(from_prev_span, kp_ref[rows_prev_span, :], k_ref[rows_prev_own, :])
                vp = jnp.where(from_prev_span, vp_ref[rows_prev_span, :], v_ref[rows_prev_own, :])
                variant = (from_prev_span & first_span).astype(jnp.int32)
            q2 = q_ref[rows, :] * scale
            k2 = jnp.concatenate([kp, k_ref[rows, :]], axis=0).astype(BF16)
            v2 = jnp.concatenate([vp, v_ref[rows, :]], axis=0).astype(BF16)
            qm = jnp.concatenate([jnp.where(lo, q2, 0.0), jnp.where(lo, 0.0, q2)], axis=0).astype(BF16)
            s = lax.dot_general(qm, k2, _NT, preferred_element_type=F32)
            s = s + bias_ref[pi, variant]
            m = jnp.max(s, axis=-1, keepdims=True)
            p = jnp.exp2(s - m)
            l = jnp.sum(p, axis=-1, keepdims=True)
            pv = jnp.dot(p.astype(BF16), v2, preferred_element_type=F32)
            m_b = jnp.where(lo, m[:L], m[L:])
            l_b = jnp.where(lo, l[:L], l[L:])
            acc_b = jnp.where(lo, pv[:L], pv[L:])
            if pi > 0:
                m_old = m_sc[rows, :]
                m_new = jnp.maximum(m_old, m_b)
                w_old = jnp.exp2(m_old - m_new)
                w_b = jnp.exp2(m_b - m_new)
                l_b = w_old * l_sc[rows, :] + w_b * l_b
                acc_b = w_old * acc_sc[rows, :] + w_b * acc_b
                m_b = m_new
            if pi < last:
                m_sc[rows, :] = m_b
                l_sc[rows, :] = l_b
                acc_sc[rows, :] = acc_b
            else:
                o_ref[rows, :] = acc_b / l_b
            return carry

        lax.fori_loop(0, span // L, unit, 0, unroll=unroll)


def _dilated(proj, bias, patterns=A_PATTERNS, unroll=8):
    bsz, t, _ = proj.shape
    L = A_BLOCK
    span = L * max(d for _, d in patterns)
    assert t % span == 0
    npair = A_WIDTH // LANES
    qc, kc, vc = A_COL0 // LANES, A_COL0 // LANES + npair, A_COL0 // LANES + 2 * npair

    def own(c):
        return pl.BlockSpec((None, span, LANES), lambda b, h, n: (b, n, c + h))

    def prev(c):
        return pl.BlockSpec((None, span, LANES), lambda b, h, n: (b, jnp.maximum(n - 1, 0), c + h))

    return pl.pallas_call(
        functools.partial(_dilated_body, span=span, patterns=patterns, unroll=unroll),
        grid=(bsz, npair, t // span),
        in_specs=[own(qc), prev(kc), own(kc), prev(vc), own(vc),
                  pl.BlockSpec((len(patterns), 2, None, 2 * L, 2 * L), lambda b, h, n: (0, 0, h, 0, 0))],
        out_specs=pl.BlockSpec((None, span, LANES), lambda b, h, n: (b, n, h)),
        out_shape=jax.ShapeDtypeStruct((bsz, t, A_WIDTH), F32),
        scratch_shapes=[pltpu.VMEM((span, LANES), F32)] * 3,
        compiler_params=pltpu.CompilerParams(
            dimension_semantics=("arbitrary", "arbitrary", "arbitrary"), vmem_limit_bytes=VMEM_LIMIT),
        name="dilated",
    )(proj, proj, proj, proj, proj,
      bias.reshape(bias.shape[:2] + (npair, 2 * L, 2 * L)))


def _cumsum_rows(tri, x):
    hi = x.astype(BF16)
    r1 = x - hi.astype(F32)
    mid = r1.astype(BF16)
    lo = (r1 - mid.astype(F32)).astype(BF16)
    return sum(jnp.dot(tri, part, preferred_element_type=F32) for part in (hi, mid, lo))


def _hgrn_gates(z, lb):
    sig = _sigmoid(z)
    fg = lb + (1.0 - lb) * sig
    return jnp.log(jnp.maximum(fg, 1e-30)) * LOG2E, (1.0 - lb) * (1.0 - sig)


def _hgrn_intra_exact(q, k, b, v, lag, sub):
    C, V = v.shape
    amat = jnp.where(lag == 0, jnp.sum(q * k, axis=-1, keepdims=True), 0.0)
    for d in range(1, sub):
        ks = pltpu.roll(k, d, 0)
        bs = pltpu.roll(b, d, 0)
        pr = q * ks * jnp.exp2(jnp.minimum(b - bs, 0.0))
        amat = jnp.where(lag == d, jnp.sum(pr, axis=-1, keepdims=True), amat)
    o = jnp.dot(amat.astype(BF16), v.astype(BF16), preferred_element_type=F32)
    for j in range(C // sub - 1):
        e = sub * (j + 1)
        r = b[e - 1:e]
        kh = (k[e - sub:e] * jnp.exp2(r - b[e - sub:e])).astype(BF16)
        qt = (q[e:] * jnp.exp2(b[e:] - r)).astype(BF16)
        a = lax.dot_general(qt, kh, _NT, preferred_element_type=F32)
        contrib = jnp.dot(a.astype(BF16), v[e - sub:e].astype(BF16), preferred_element_type=F32)
        o = o + jnp.concatenate([jnp.zeros((e, V), F32), contrib], axis=0)
    return o


def _hgrn_scores_factored(q, k, b, causal, sub):
    C, K = q.shape
    qs, ks = [], []
    for j in range(C // sub):
        s0 = sub * j
        r = b[s0 - 1:s0] if j > 0 else jnp.zeros((1, K), F32)
        qt = (q[s0:] * jnp.exp2(b[s0:] - r)).astype(BF16)
        kh = (k[s0:s0 + sub] * jnp.exp2(r - b[s0:s0 + sub])).astype(BF16)
        above = [jnp.zeros((s0, K), BF16)] if s0 > 0 else []
        below = [jnp.zeros((C - s0 - sub, K), BF16)] if C - s0 - sub > 0 else []
        qs.append(jnp.concatenate(above + [qt], axis=0))
        ks.append(jnp.concatenate(above + [kh] + below, axis=0))
    a = lax.dot_general(jnp.concatenate(qs, axis=1), jnp.concatenate(ks, axis=1), _NT,
                        preferred_element_type=F32)
    return jnp.where(causal, a, 0.0)


def _hgrn_body(q_ref, f_ref, i_ref, g_ref, lb_ref, gn_ref, o_ref, st_ref, *, chunk, sub, fsub, nchunk):
    C = chunk
    K, V = B_KEY_DIM, B_VAL_DIM

    @pl.when(pl.program_id(1) == 0)
    def _():
        st_ref[...] = jnp.zeros_like(st_ref)

    ri = lax.broadcasted_iota(jnp.int32, (C, C), 0)
    ci = lax.broadcasted_iota(jnp.int32, (C, C), 1)
    causal = ri >= ci
    tri = causal.astype(F32).astype(BF16)
    lag = jnp.where((ri // sub == ci // sub) & causal, ri - ci, -1)

    def run(factored):
        def chunk_body(ic, carry):
            r0 = pl.multiple_of(ic * C, C)
            rows = pl.ds(r0, C)
            outs = []
            g2_all, k_all = _hgrn_gates(f_ref[0, rows, :], lb_ref[...])
            b_all = _cumsum_rows(tri, g2_all)
            for h in range(B_HEADS):
                kc = slice(h * K, (h + 1) * K)
                vc = slice(h * V, (h + 1) * V)
                qraw = q_ref[0, rows, kc]
                v = i_ref[0, rows, vc]
                gate = g_ref[0, rows, vc]
                k, b = k_all[:, kc], b_all[:, kc]
                q = qraw * _sigmoid(qraw)
                st = st_ref[h]
                vt = v.T.astype(BF16)
                qe = (q * jnp.exp2(b)).astype(BF16)
                o = lax.dot_general(qe, st.astype(BF16), _NT, preferred_element_type=F32)
                if factored:
                    a = _hgrn_scores_factored(q, k, b, causal, fsub).astype(BF16)
                    o = o + jnp.dot(a, v.astype(BF16), preferred_element_type=F32)
                else:
                    o = o + _hgrn_intra_exact(q, k, b, v, lag, sub)
                bl = b[C - 1:C]
                kd = (k * jnp.exp2(bl - b)).astype(BF16)
                st_ref[h] = st * jnp.exp2(bl) + jnp.dot(vt, kd, preferred_element_type=F32)
                on = _rms(o, gn_ref[...])
                outs.append(on * (gate * _sigmoid(gate)))
            o_ref[0, rows, :] = jnp.concatenate(outs, axis=-1)
            return carry

        lax.fori_loop(0, nchunk, chunk_body, 0, unroll=4 if factored else 1)

    g2_all, _ = _hgrn_gates(f_ref[0], lb_ref[...])
    tb = g2_all.shape[0]
    worst = jnp.max(-jnp.sum(g2_all.reshape(tb // fsub, fsub, g2_all.shape[1]), axis=1))
    lax.cond(worst < HGRN_FACTOR_LIMIT, lambda: run(True), lambda: run(False))


def _hgrn(proj, lb, gnorm, tb=256):
    bsz, t, _ = proj.shape
    C = HGRN_CHUNK
    assert t % tb == 0 and tb % C == 0
    kw, vw = B_QK_WIDTH, B_V_WIDTH
    return pl.pallas_call(
        functools.partial(_hgrn_body, chunk=C, sub=HGRN_SUB, fsub=HGRN_FSUB, nchunk=tb // C),
        grid=(bsz, t // tb),
        in_specs=[
            pl.BlockSpec((1, tb, kw), lambda b, i: (b, i, 0)),
            pl.BlockSpec((1, tb, kw), lambda b, i: (b, i, 1)),
            pl.BlockSpec((1, tb, vw), lambda b, i: (b, i, 2 * kw // vw)),
            pl.BlockSpec((1, tb, vw), lambda b, i: (b, i, 2 * kw // vw + 1)),
            pl.BlockSpec((1, kw), lambda b, i: (0, 0)),
            pl.BlockSpec((1, B_VAL_DIM), lambda b, i: (0, 0)),
        ],
        out_specs=pl.BlockSpec((1, tb, vw), lambda b, i: (b, i, 0)),
        out_shape=jax.ShapeDtypeStruct((bsz, t, vw), F32),
        scratch_shapes=[pltpu.VMEM((B_HEADS, B_VAL_DIM, B_KEY_DIM), F32)],
        compiler_params=pltpu.CompilerParams(
            dimension_semantics=("arbitrary", "arbitrary"), vmem_limit_bytes=VMEM_LIMIT),
        name="hgrn2",
    )(proj, proj, proj, proj, lb, gnorm)


def _moba_select_body(q_ref, k_ref, qx_ref, kmean, *, nb, topk, qchunk):
    L = C_BLOCK
    E = HEAD_DIM
    G = LANES // 4
    kmean[...] = jnp.zeros_like(kmean)
    for n in range(nb):
        kmean[n:n + 1, :] = jnp.mean(k_ref[0, n * L:(n + 1) * L, :], axis=0, keepdims=True)
    lane = lax.broadcasted_iota(jnp.int32, (qchunk, LANES), 1)
    grp = lane // G
    blk_i = lax.broadcasted_iota(jnp.int32, (G, qchunk), 0)
    blk_f = blk_i.astype(F32)
    qcol = lax.broadcasted_iota(jnp.int32, (G, qchunk), 1)
    place_r = lax.broadcasted_iota(jnp.int32, (G, LANES), 0)
    place_c = lax.broadcasted_iota(jnp.int32, (G, LANES), 1)
    qscale = (E ** -0.5) * LOG2E

    def chunk_body(c, carry):
        r0 = pl.multiple_of(c * qchunk, qchunk)
        q2 = q_ref[0, pl.ds(r0, qchunk), :]
        qblk = (qcol + r0) // L
        for half in range(2):
            hmask = (lane < E) if half == 0 else (lane >= E)
            qf = jnp.where(hmask, q2, 0.0)
            gate = lax.dot_general(kmean[...], qf, _NT, preferred_element_type=F32,
                                   precision=lax.Precision.HIGHEST)
            gate = jnp.where(blk_i < qblk, gate, NEG_BIG)
            picked = jnp.zeros((G, qchunk), jnp.bool_)
            for _ in range(topk):
                mx = jnp.max(gate, axis=0, keepdims=True)
                ix = jnp.min(jnp.where(gate == mx, blk_f, float(G)), axis=0, keepdims=True)
                hit = blk_f == ix
                picked = picked | (hit & (mx > 0.5 * NEG_BIG))
                gate = jnp.where(hit, -jnp.inf, gate)
            wanted = picked | (blk_i == qblk)
            sel_grp = 2 if half == 0 else 0
            place = (place_c == place_r + sel_grp * G).astype(BF16)
            flags = jnp.dot(wanted.astype(F32).T.astype(BF16), place, preferred_element_type=F32)
            ext = jnp.where((grp == sel_grp + 1) | ((grp == sel_grp) & (flags < 0.5)), NEG_BIG, 0.0)
            qx_ref[0, half, pl.ds(r0, qchunk), :] = jnp.where(hmask, qf * qscale, ext).astype(BF16)
        return carry

    lax.fori_loop(0, q_ref.shape[1] // qchunk, chunk_body, 0)


def _moba_body(qx, k_ref, v_ref, bias_ref, o_ref, kbh, vtb, sbuf, pbuf, acc,
               *, nb, ntile, unroll):
    L = C_BLOCK
    E = HEAD_DIM
    G = LANES // 4
    U = unroll
    assert U >= 2
    qb = pl.program_id(2)
    lane = lax.broadcasted_iota(jnp.int32, (L, LANES), 1)

    @pl.when(qb == 0)
    def _():
        ones_rows = jnp.ones((VT_ROWS - E, L), BF16)
        for n in range(nb):
            blk = k_ref[0, n * L:(n + 1) * L, :]
            kbh[n, 0] = jnp.where(lane < E, blk, (lane == 2 * G + n).astype(F32)).astype(BF16)
            kbh[n, 1] = jnp.where(lane >= E, blk, (lane == n).astype(F32)).astype(BF16)
            vt = v_ref[0, n * L:(n + 1) * L, :].T.astype(BF16)
            for half in range(2):
                vtb[n, half, 0:E, :] = vt[half * E:(half + 1) * E, :]
                vtb[n, half, E:, :] = ones_rows
        kbh[nb, 0] = (lane == 3 * G).astype(BF16)
        kbh[nb, 1] = (lane == G).astype(BF16)

    def scores_stage(dist, slot_):
        kb = jnp.where(dist <= qb, qb - dist, nb)
        tile = jnp.minimum(dist, ntile - 1)
        for half in range(2):
            s = lax.dot_general(kbh[kb, half], qx[0, half], _NT, preferred_element_type=F32)
            sbuf[slot_, half] = s + bias_ref[half, tile]

    def softmax_stage(slot_, m_old):
        out = []
        for half in range(2):
            s = sbuf[slot_, half]
            m_new = jnp.maximum(m_old[half], jnp.max(s, axis=0, keepdims=True))
            pbuf[slot_, half] = jnp.exp2(s - m_new).astype(BF16)
            out.append(m_new)
        return tuple(out)

    def pv_stage(dist, slot_, alpha):
        kb = jnp.clip(qb - dist, 0, nb - 1)
        for half in range(2):
            acc[half] = alpha[half] * acc[half] + jnp.dot(vtb[kb, half], pbuf[slot_, half],
                                                          preferred_element_type=F32)

    pbuf[...] = jnp.zeros_like(pbuf)
    acc[...] = jnp.zeros_like(acc)
    scores_stage(0, 0)

    def trip(j, carry):
        m_old, alpha = carry[0:2], carry[2:4]
        for u in range(U):
            dist = j * U + u
            pv_stage(dist - 1, (u - 1) % U, alpha)
            m_new = softmax_stage(u, m_old)
            scores_stage(dist + 1, (u + 1) % U)
            alpha = tuple(jnp.exp2(a - b) for a, b in zip(m_old, m_new))
            m_old = m_new
        return m_old + alpha

    row = jnp.full((1, L), NEG_BIG, F32)
    one = jnp.ones((1, L), F32)
    ntrips = (qb + U) // U
    carry = lax.fori_loop(0, ntrips, trip, (row, row, one, one))
    pv_stage(ntrips * U - 1, (U - 1) % U, carry[2:4])
    out_t = jnp.concatenate([acc[half, 0:E, :] / acc[half, E:E + 1, :] for half in range(2)], axis=0)
    o_ref[0] = out_t.T


def _moba(proj, bias_tiles, unroll=2):
    bsz, t, _ = proj.shape
    L = C_BLOCK
    assert t % L == 0
    nb = t // L
    assert nb <= LANES // 4, "one selection lane per key block and lane group"
    ntile = bias_tiles.shape[1]
    qcol, kcol, vcol = (C_COL0 // LANES, (C_COL0 + C_WIDTH) // LANES, (C_COL0 + 2 * C_WIDTH) // LANES)
    npair = C_HEADS // 2
    qchunk = min(2048, t)
    assert t % qchunk == 0
    qx = pl.pallas_call(
        functools.partial(_moba_select_body, nb=nb, topk=min(C_TOPK, nb), qchunk=qchunk),
        grid=(bsz, npair),
        in_specs=[
            pl.BlockSpec((1, t, LANES), lambda b, h: (b, 0, qcol + h)),
            pl.BlockSpec((1, t, LANES), lambda b, h: (b, 0, kcol + h)),
        ],
        out_specs=pl.BlockSpec((1, 2, t, LANES), lambda b, h: (b, h, 0, 0)),
        out_shape=jax.ShapeDtypeStruct((bsz, C_HEADS, t, LANES), BF16),
        scratch_shapes=[pltpu.VMEM((LANES // 4, LANES), F32)],
        compiler_params=pltpu.CompilerParams(
            dimension_semantics=("arbitrary", "arbitrary"), vmem_limit_bytes=VMEM_LIMIT),
        name="moba_select",
    )(proj, proj)
    return pl.pallas_call(
        functools.partial(_moba_body, nb=nb, ntile=ntile, unroll=unroll),
        grid=(bsz, npair, nb),
        in_specs=[
            pl.BlockSpec((1, 2, L, LANES), lambda b, h, i: (b, h, i, 0)),
            pl.BlockSpec((1, t, LANES), lambda b, h, i: (b, 0, kcol + h)),
            pl.BlockSpec((1, t, LANES), lambda b, h, i: (b, 0, vcol + h)),
            pl.BlockSpec((2, ntile, L, L), lambda b, h, i: (h, 0, 0, 0)),
        ],
        out_specs=pl.BlockSpec((1, L, LANES), lambda b, h, i: (b, i, h)),
        out_shape=jax.ShapeDtypeStruct((bsz, t, C_WIDTH), F32),
        scratch_shapes=[
            pltpu.VMEM((nb + 1, 2, L, LANES), BF16),
            pltpu.VMEM((nb, 2, VT_ROWS, L), BF16),
            pltpu.VMEM((unroll, 2, L, L), F32),
            pltpu.VMEM((unroll, 2, L, L), BF16),
            pltpu.VMEM((2, VT_ROWS, L), F32),
        ],
        compiler_params=pltpu.CompilerParams(
            dimension_semantics=("arbitrary", "arbitrary", "arbitrary"),
            vmem_limit_bytes=VMEM_LIMIT),
        name="moba",
    )(qx, proj, proj, bias_tiles)


def _post_body(x_ref, ya_ref, yb_ref, yc_ref, p_ref,
               wout_ref, nffn_ref, wup_ref, cw_ref, cb_ref, wdown_ref, nple_ref, wpg_ref, wpe_ref,
               nfin_ref, out_ref, carry_ref, *, final, fch):
    tm = x_ref.shape[1]

    @pl.when(pl.program_id(1) == 0)
    def _():
        carry_ref[...] = jnp.zeros_like(carry_ref)

    x = x_ref[0]
    x = x + jnp.dot(ya_ref[0].astype(BF16), wout_ref[0:A_WIDTH, :], preferred_element_type=F32)
    x = x + jnp.dot(yb_ref[0].astype(BF16), wout_ref[A_WIDTH:A_WIDTH + B_V_WIDTH, :],
                    preferred_element_type=F32)
    x = x + jnp.dot(yc_ref[0].astype(BF16), wout_ref[A_WIDTH + B_V_WIDTH:, :],
                    preferred_element_type=F32)

    h = _rms(x, nffn_ref[...]).astype(BF16)
    row = lax.broadcasted_iota(jnp.int32, (tm, 1), 0)

    def conv_cols(c0):
        cols = slice(c0, c0 + fch)
        u = jnp.dot(h, wup_ref[:, cols], preferred_element_type=F32)
        prev = carry_ref[:, cols]
        carry_ref[:, cols] = u[tm - 8:tm]
        u1 = jnp.where(row == 0, prev[7:8], pltpu.roll(u, 1, 0))
        u2 = jnp.where(row == 0, prev[6:7], jnp.where(row == 1, prev[7:8], pltpu.roll(u, 2, 0)))
        w = cw_ref[:, cols]
        return w[0:1] * u2 + w[1:2] * u1 + w[2:3] * u + cb_ref[:, cols]

    acc = jnp.zeros(x.shape, F32)
    for j in range(D_FF // fch):
        ug = conv_cols(j * fch)
        uv = conv_cols(D_FF + j * fch)
        act = (ug * _sigmoid(ug)) * uv
        acc = acc + jnp.dot(act.astype(BF16), wdown_ref[j * fch:(j + 1) * fch, :],
                            preferred_element_type=F32)
    x = x + acc

    hp = _rms(x, nple_ref[...]).astype(BF16)
    gate = _sigmoid(jnp.dot(hp, wpg_ref[...], preferred_element_type=F32))
    x = x + gate * jnp.dot(p_ref[0].astype(BF16), wpe_ref[...], preferred_element_type=F32)
    if final:
        x = _rms(x, nfin_ref[...])
    out_ref[0] = x


def _post(x, ya, yb, yc, p, wout, nffn, wup, cw, cb, wdown, nple, wpg, wpe, nfin, final,
          tm=512, fch=2816):
    bsz, t, d = x.shape
    assert t % tm == 0 and D_FF % fch == 0

    def tok(width):
        return pl.BlockSpec((1, tm, width), lambda b, i: (b, i, 0))

    def const(shape):
        return pl.BlockSpec(shape, lambda b, i: (0,) * len(shape), pipeline_mode=pl.Buffered(1))

    return pl.pallas_call(
        functools.partial(_post_body, final=final, fch=fch),
        grid=(bsz, t // tm),
        in_specs=[tok(d), tok(A_WIDTH), tok(B_V_WIDTH), tok(C_WIDTH), tok(PLE_DIM),
                  const(wout.shape), const(nffn.shape), const(wup.shape), const(cw.shape),
                  const(cb.shape), const(wdown.shape), const(nple.shape), const(wpg.shape),
                  const(wpe.shape), const(nfin.shape)],
        out_specs=tok(d),
        out_shape=jax.ShapeDtypeStruct((bsz, t, d), F32),
        scratch_shapes=[pltpu.VMEM((8, 2 * D_FF), F32)],
        compiler_params=pltpu.CompilerParams(
            dimension_semantics=("arbitrary", "arbitrary"), vmem_limit_bytes=VMEM_LIMIT),
        name="post",
    )(x, ya, yb, yc, p, wout, nffn, wup, cw, cb, wdown, nple, wpg, wpe, nfin)


def _dilated_bias(bias_a, window, dilation):
    L = A_BLOCK
    m = jnp.arange(3 * L - 1)
    u = bias_a[:, _rel_bucket((2 * L - 1 - m) * dilation)]
    tile = _toeplitz(u, L, 2 * L) * LOG2E
    qi = jnp.arange(L)[:, None]
    kj = jnp.arange(2 * L)[None, :]
    delta = qi + L - kj
    band = (delta >= 0) & (delta <= window // dilation)
    return jnp.stack([jnp.where(band, tile, NEG_BIG), jnp.where(band & (kj >= L), tile, NEG_BIG)])


def _moba_bias(bias_c, t):
    L = C_BLOCK
    nb = t // L
    dist = np.arange(0, t)
    bk = _rel_bucket_np(dist)
    nfar = nb
    for dblk in range(nb, 0, -1):
        if np.all(bk[max(dblk * L - (L - 1), 0):] == bk[-1]):
            nfar = dblk
    m = jnp.arange(2 * L - 1)
    tiles = []
    for dblk in range(nfar):
        u = bias_c[:, _rel_bucket(dblk * L + m - (L - 1))]
        tiles.append(_toeplitz(u, L, L))
    far = jnp.broadcast_to(bias_c[:, int(bk[-1])][:, None, None], (bias_c.shape[0], L, L))
    tiles.append(far)
    kj = jnp.arange(L)[:, None]
    qi = jnp.arange(L)[None, :]
    tiles[0] = jnp.where(qi >= kj, tiles[0] * LOG2E, NEG_BIG)
    tiles[1:] = [tl * LOG2E for tl in tiles[1:]]
    return jnp.stack(tiles, axis=1)


def kernel(x, p, rel_bias, norm_mix, w_in, w_out, lower_bounds, hgrn_gnorm, norm_ffn, w_up, conv_w,
           conv_b, w_down, norm_ple, w_pe, w_pg, norm_final):
    bsz, t, d = x.shape
    depth = w_in.shape[0]
    lb_sm = jax.nn.softmax(lower_bounds.astype(F32), axis=0)
    lb_all = jnp.cumsum(lb_sm, axis=0) - lb_sm[0]
    bias_a = rel_bias[:A_HEADS].astype(F32)
    bias_c = rel_bias[A_HEADS:].astype(F32)
    bias_a_tiles = jnp.stack([_dilated_bias(bias_a, win, dil) for win, dil in A_PATTERNS])
    bias_c_tiles = _moba_bias(bias_c, t)

    a_end = 3 * A_WIDTH
    b_end = a_end + B_GROUP_WIDTH
    for i in range(depth):
        w = w_in[i]
        w_perm = jnp.concatenate([w[:, a_end:b_end], w[:, :a_end], w[:, b_end:]], axis=1).astype(BF16)
        proj = _inproj(x.reshape(bsz * t, d), norm_mix[i][None], w_perm).reshape(bsz, t, IN_WIDTH)
        ya = _dilated(proj, bias_a_tiles)
        yb = _hgrn(proj, lb_all[i][None], hgrn_gnorm[i].astype(F32)[None])
        yc = _moba(proj, bias_c_tiles)
        x = _post(x, ya, yb, yc, p[i], w_out[i].astype(BF16), norm_ffn[i][None],
                  w_up[i].astype(BF16), conv_w[i], conv_b[i][None], w_down[i].astype(BF16),
                  norm_ple[i][None], w_pg[i].astype(BF16), w_pe[i].astype(BF16), norm_final[None],
                  final=(i == depth - 1))
    return x
```

```python
import functools
import math

import numpy as np
import jax
import jax.numpy as jnp
from jax import lax
from jax.experimental import pallas as pl
from jax.experimental.pallas import tpu as pltpu

F32 = jnp.float32
BF16 = jnp.bfloat16

HEAD_DIM = 64
A_HEADS = 6
A_PATTERNS = ((2048, 16), (512, 4), (128, 1))
A_BLOCK = 128
B_HEADS = 4
B_KEY_DIM = 128
B_VAL_DIM = 64
C_HEADS = 6
C_BLOCK = 256
C_TOPK = 3
N_BUCKETS = 32
MAX_DISTANCE = 2048
D_FF = 2816
PLE_DIM = 256
EPS = 1e-6
NEG_BIG = -1e30

A_WIDTH = A_HEADS * HEAD_DIM
B_QK_WIDTH = B_HEADS * B_KEY_DIM
B_V_WIDTH = B_HEADS * B_VAL_DIM
C_WIDTH = C_HEADS * HEAD_DIM
IN_WIDTH = 3 * A_WIDTH + 2 * B_QK_WIDTH + 2 * B_V_WIDTH + 3 * C_WIDTH
LANES = 128

B_GROUP_WIDTH = 2 * B_QK_WIDTH + 2 * B_V_WIDTH
A_COL0 = B_GROUP_WIDTH
C_COL0 = A_COL0 + 3 * A_WIDTH

LOG2E = 1.4426950408889634
VT_ROWS = HEAD_DIM + 16

HGRN_CHUNK = 64
HGRN_SUB = 8
HGRN_FSUB = 16
HGRN_FACTOR_LIMIT = 64.0
VMEM_LIMIT = 56 * 1024 * 1024

_NT = (((1,), (1,)), ((), ()))


def _rel_bucket(dist):
    max_exact = N_BUCKETS // 2
    d = jnp.maximum(dist, 0)
    df = jnp.maximum(d, max_exact).astype(F32)
    large = max_exact + (jnp.log(df / max_exact) / math.log(MAX_DISTANCE / max_exact)
                         * (N_BUCKETS - max_exact)).astype(jnp.int32)
    large = jnp.minimum(large, N_BUCKETS - 1)
    return jnp.where(d < max_exact, d, large)


def _rel_bucket_np(dist):
    max_exact = N_BUCKETS // 2
    d = np.maximum(dist, 0)
    df = np.maximum(d, max_exact).astype(np.float64)
    large = max_exact + (np.log(df / max_exact) / math.log(MAX_DISTANCE / max_exact)
                         * (N_BUCKETS - max_exact)).astype(np.int64)
    large = np.minimum(large, N_BUCKETS - 1)
    return np.where(d < max_exact, d, large)


def _toeplitz(u, rows, cols):
    n = rows + cols - 1
    w = jnp.concatenate([u, jnp.zeros(u.shape[:-1] + (1,), u.dtype)], axis=-1)
    flat = jnp.tile(w, rows)[..., : rows * n]
    m = flat.reshape(u.shape[:-1] + (rows, n))
    return m[..., rows - 1: rows - 1 + cols]


def _rms(x, g):
    return x * lax.rsqrt(jnp.mean(x * x, axis=-1, keepdims=True) + EPS) * g


def _sigmoid(x):
    return 1.0 / (1.0 + jnp.exp(-x))


def _inproj_body(x_ref, g_ref, w_ref, o_ref, *, nch):
    h = _rms(x_ref[...], g_ref[...]).astype(BF16)
    for j in range(0, w_ref.shape[1], nch):
        o_ref[:, j:j + nch] = jnp.dot(h, w_ref[:, j:j + nch], preferred_element_type=F32)


def _inproj(x2, gain, w_bf, tm=512, nch=768):
    n, d = x2.shape
    width = w_bf.shape[1]
    return pl.pallas_call(
        functools.partial(_inproj_body, nch=nch),
        grid=(n // tm,),
        in_specs=[
            pl.BlockSpec((tm, d), lambda i: (i, 0)),
            pl.BlockSpec((1, d), lambda i: (0, 0)),
            pl.BlockSpec((d, width), lambda i: (0, 0), pipeline_mode=pl.Buffered(1)),
        ],
        out_specs=pl.BlockSpec((tm, width), lambda i: (i, 0)),
        out_shape=jax.ShapeDtypeStruct((n, width), F32),
        compiler_params=pltpu.CompilerParams(
            dimension_semantics=("arbitrary",), vmem_limit_bytes=VMEM_LIMIT),
        name="inproj",
    )(x2, gain, w_bf)


def _dilated_body(q_ref, kp_ref, k_ref, vp_ref, v_ref, bias_ref, o_ref, m_sc, l_sc, acc_sc,
                  *, span, patterns, unroll):
    L = A_BLOCK
    first_span = pl.program_id(2) == 0
    lo = lax.broadcasted_iota(jnp.int32, (L, LANES), 1) < HEAD_DIM
    scale = (HEAD_DIM ** -0.5) * LOG2E
    last = len(patterns) - 1

    for pi, (window, d) in enumerate(patterns):
        nblk = span // (L * d)

        def rows_at(start, d=d):
            return pl.ds(start, L) if d == 1 else pl.ds(start, L, stride=d)

        def unit(u, carry, pi=pi, d=d, nblk=nblk, rows_at=rows_at):
            r = u % d
            n = u // d
            rows = rows_at(r + d * L * n)
            rows_prev_span = rows_at(r + d * L * (nblk - 1))
            if nblk == 1:
                kp, vp = kp_ref[rows_prev_span, :], vp_ref[rows_prev_span, :]
                variant = first_span.astype(jnp.int32)
            else:
                rows_prev_own = rows_at(r + d * L * jnp.maximum(n - 1, 0))
                from_prev_span = n == 0
                kp = jnp.where(from_prev_span, kp_ref[rows_prev_span, :], k_ref[rows_prev_own, :])
                vp = jnp.where(from_prev_span, vp_ref[rows_prev_span, :], v_ref[rows_prev_own, :])
                variant = (from_prev_span & first_span).astype(jnp.int32)
            q2 = q_ref[rows, :] * scale
            k2 = jnp.concatenate([kp, k_ref[rows, :]], axis=0).astype(BF16)
            v2 = jnp.concatenate([vp, v_ref[rows, :]], axis=0).astype(BF16)
            qm = jnp.concatenate([jnp.where(lo, q2, 0.0), jnp.where(lo, 0.0, q2)], axis=0).astype(BF16)
            s = lax.dot_general(qm, k2, _NT, preferred_element_type=F32)
            s = s + bias_ref[pi, variant]
            m = jnp.max(s, axis=-1, keepdims=True)
            p = jnp.exp2(s - m)
            l = jnp.sum(p, axis=-1, keepdims=True)
            pv = jnp.dot(p.astype(BF16), v2, preferred_element_type=F32)
            m_b = jnp.where(lo, m[:L], m[L:])
            l_b = jnp.where(lo, l[:L], l[L:])
            acc_b = jnp.where(lo, pv[:L], pv[L:])
            if pi > 0:
                m_old = m_sc[rows, :]
                m_new = jnp.maximum(m_old, m_b)
                w_old = jnp.exp2(m_old - m_new)
                w_b = jnp.exp2(m_b - m_new)
                l_b = w_old * l_sc[rows, :] + w_b * l_b
                acc_b = w_old * acc_sc[rows, :] + w_b * acc_b
                m_b = m_new
            if pi < last:
                m_sc[rows, :] = m_b
                l_sc[rows, :] = l_b
                acc_sc[rows, :] = acc_b
            else:
                o_ref[rows, :] = acc_b / l_b
            return carry

        lax.fori_loop(0, span // L, unit, 0, unroll=unroll)


def _dilated(proj, bias, patterns=A_PATTERNS, unroll=16):
    bsz, t, _ = proj.shape
    L = A_BLOCK
    span = L * max(d for _, d in patterns)
    assert t % span == 0
    npair = A_WIDTH // LANES
    qc, kc, vc = A_COL0 // LANES, A_COL0 // LANES + npair, A_COL0 // LANES + 2 * npair

    def own(c):
        return pl.BlockSpec((None, span, LANES), lambda b, h, n: (b, n, c + h))

    def prev(c):
        return pl.BlockSpec((None, span, LANES), lambda b, h, n: (b, jnp.maximum(n - 1, 0), c + h))

    return pl.pallas_call(
        functools.partial(_dilated_body, span=span, patterns=patterns, unroll=unroll),
        grid=(bsz, npair, t // span),
        in_specs=[own(qc), prev(kc), own(kc), prev(vc), own(vc),
                  pl.BlockSpec((len(patterns), 2, None, 2 * L, 2 * L), lambda b, h, n: (0, 0, h, 0, 0))],
        out_specs=pl.BlockSpec((None, span, LANES), lambda b, h, n: (b, n, h)),
        out_shape=jax.ShapeDtypeStruct((bsz, t, A_WIDTH), F32),
        scratch_shapes=[pltpu.VMEM((span, LANES), F32)] * 3,
        compiler_params=pltpu.CompilerParams(
            dimension_semantics=("arbitrary", "arbitrary", "arbitrary"), vmem_limit_bytes=VMEM_LIMIT),
        name="dilated",
    )(proj, proj, proj, proj, proj,
      bias.reshape(bias.shape[:2] + (npair, 2 * L, 2 * L)))


def _cumsum_rows(tri, x):
    hi = x.astype(BF16)
    r1 = x - hi.astype(F32)
    mid = r1.astype(BF16)
    lo = (r1 - mid.astype(F32)).astype(BF16)
    return sum(jnp.dot(tri, part, preferred_element_type=F32) for part in (hi, mid, lo))


def _hgrn_gates(z, lb):
    sig = _sigmoid(z)
    fg = lb + (1.0 - lb) * sig
    return jnp.log(jnp.maximum(fg, 1e-30)) * LOG2E, (1.0 - lb) * (1.0 - sig)


def _hgrn_intra_exact(q, k, b, v, lag, sub):
    C, V = v.shape
    amat = jnp.where(lag == 0, jnp.sum(q * k, axis=-1, keepdims=True), 0.0)
    for d in range(1, sub):
        ks = pltpu.roll(k, d, 0)
        bs = pltpu.roll(b, d, 0)
        pr = q * ks * jnp.exp2(jnp.minimum(b - bs, 0.0))
        amat = jnp.where(lag == d, jnp.sum(pr, axis=-1, keepdims=True), amat)
    o = jnp.dot(amat.astype(BF16), v.astype(BF16), preferred_element_type=F32)
    for j in range(C // sub - 1):
        e = sub * (j + 1)
        r = b[e - 1:e]
        kh = (k[e - sub:e] * jnp.exp2(r - b[e - sub:e])).astype(BF16)
        qt = (q[e:] * jnp.exp2(b[e:] - r)).astype(BF16)
        a = lax.dot_general(qt, kh, _NT, preferred_element_type=F32)
        contrib = jnp.dot(a.astype(BF16), v[e - sub:e].astype(BF16), preferred_element_type=F32)
        o = o + jnp.concatenate([jnp.zeros((e, V), F32), contrib], axis=0)
    return o


def _hgrn_scores_factored(q, k, b, causal, sub):
    C, K = q.shape
    qs, ks = [], []
    for j in range(C // sub):
        s0 = sub * j
        r = b[s0 - 1:s0] if j > 0 else jnp.zeros((1, K), F32)
        qt = (q[s0:] * jnp.exp2(b[s0:] - r)).astype(BF16)
        kh = (k[s0:s0 + sub] * jnp.exp2(r - b[s0:s0 + sub])).astype(BF16)
        above = [jnp.zeros((s0, K), BF16)] if s0 > 0 else []
        below = [jnp.zeros((C - s0 - sub, K), BF16)] if C - s0 - sub > 0 else []
        qs.append(jnp.concatenate(above + [qt], axis=0))
        ks.append(jnp.concatenate(above + [kh] + below, axis=0))
    a = lax.dot_general(jnp.concatenate(qs, axis=1), jnp.concatenate(ks, axis=1), _NT,
                        preferred_element_type=F32)
    return jnp.where(causal, a, 0.0)


def _hgrn_body(q_ref, f_ref, i_ref, g_ref, lb_ref, gn_ref, o_ref, st_ref, *, chunk, sub, fsub, nchunk):
    C = chunk
    K, V = B_KEY_DIM, B_VAL_DIM

    @pl.when(pl.program_id(1) == 0)
    def _():
        st_ref[...] = jnp.zeros_like(st_ref)

    ri = lax.broadcasted_iota(jnp.int32, (C, C), 0)
    ci = lax.broadcasted_iota(jnp.int32, (C, C), 1)
    causal = ri >= ci
    tri = causal.astype(F32).astype(BF16)
    lag = jnp.where((ri // sub == ci // sub) & causal, ri - ci, -1)

    def run(factored):
        def chunk_body(ic, carry):
            r0 = pl.multiple_of(ic * C, C)
            rows = pl.ds(r0, C)
            outs = []
            g2_all, k_all = _hgrn_gates(f_ref[0, rows, :], lb_ref[...])
            b_all = _cumsum_rows(tri, g2_all)
            for h in range(B_HEADS):
                kc = slice(h * K, (h + 1) * K)
                vc = slice(h * V, (h + 1) * V)
                qraw = q_ref[0, rows, kc]
                v = i_ref[0, rows, vc]
                gate = g_ref[0, rows, vc]
                k, b = k_all[:, kc], b_all[:, kc]
                q = qraw * _sigmoid(qraw)
                st = st_ref[h]
                vt = v.T.astype(BF16)
                qe = (q * jnp.exp2(b)).astype(BF16)
                o = lax.dot_general(qe, st.astype(BF16), _NT, preferred_element_type=F32)
                if factored:
                    a = _hgrn_scores_factored(q, k, b, causal, fsub).astype(BF16)
                    o = o + jnp.dot(a, v.astype(BF16), preferred_element_type=F32)
                else:
                    o = o + _hgrn_intra_exact(q, k, b, v, lag, sub)
                bl = b[C - 1:C]
                kd = (k * jnp.exp2(bl - b)).astype(BF16)
                st_ref[h] = st * jnp.exp2(bl) + jnp.dot(vt, kd, preferred_element_type=F32)
                on = _rms(o, gn_ref[...])
                outs.append(on * (gate * _sigmoid(gate)))
            o_ref[0, rows, :] = jnp.concatenate(outs, axis=-1)
            return carry

        lax.fori_loop(0, nchunk, chunk_body, 0, unroll=4 if factored else 1)

    g2_all, _ = _hgrn_gates(f_ref[0], lb_ref[...])
    tb = g2_all.shape[0]
    worst = jnp.max(-jnp.sum(g2_all.reshape(tb // fsub, fsub, g2_all.shape[1]), axis=1))
    lax.cond(worst < HGRN_FACTOR_LIMIT, lambda: run(True), lambda: run(False))


def _hgrn(proj, lb, gnorm, tb=256):
    bsz, t, _ = proj.shape
    C = HGRN_CHUNK
    assert t % tb == 0 and tb % C == 0
    kw, vw = B_QK_WIDTH, B_V_WIDTH
    return pl.pallas_call(
        functools.partial(_hgrn_body, chunk=C, sub=HGRN_SUB, fsub=HGRN_FSUB, nchunk=tb // C),
        grid=(bsz, t // tb),
        in_specs=[
            pl.BlockSpec((1, tb, kw), lambda b, i: (b, i, 0)),
            pl.BlockSpec((1, tb, kw), lambda b, i: (b, i, 1)),
            pl.BlockSpec((1, tb, vw), lambda b, i: (b, i, 2 * kw // vw)),
            pl.BlockSpec((1, tb, vw), lambda b, i: (b, i, 2 * kw // vw + 1)),
            pl.BlockSpec((1, kw), lambda b, i: (0, 0)),
            pl.BlockSpec((1, B_VAL_DIM), lambda b, i: (0, 0)),
        ],
        out_specs=pl.BlockSpec((1, tb, vw), lambda b, i: (b, i, 0)),
        out_shape=jax.ShapeDtypeStruct((bsz, t, vw), F32),
        scratch_shapes=[pltpu.VMEM((B_HEADS, B_VAL_DIM, B_KEY_DIM), F32)],
        compiler_params=pltpu.CompilerParams(
            dimension_semantics=("arbitrary", "arbitrary"), vmem_limit_bytes=VMEM_LIMIT),
        name="hgrn2",
    )(proj, proj, proj, proj, lb, gnorm)


def _moba_select_body(q_ref, k_ref, qx_ref, kmean, *, nb, topk, qchunk):
    L = C_BLOCK
    E = HEAD_DIM
    G = LANES // 4
    kmean[...] = jnp.zeros_like(kmean)
    for n in range(nb):
        kmean[n:n + 1, :] = jnp.mean(k_ref[0, n * L:(n + 1) * L, :], axis=0, keepdims=True)
    lane = lax.broadcasted_iota(jnp.int32, (qchunk, LANES), 1)
    grp = lane // G
    blk_i = lax.broadcasted_iota(jnp.int32, (G, qchunk), 0)
    blk_f = blk_i.astype(F32)
    qcol = lax.broadcasted_iota(jnp.int32, (G, qchunk), 1)
    place_r = lax.broadcasted_iota(jnp.int32, (G, LANES), 0)
    place_c = lax.broadcasted_iota(jnp.int32, (G, LANES), 1)
    qscale = (E ** -0.5) * LOG2E

    def chunk_body(c, carry):
        r0 = pl.multiple_of(c * qchunk, qchunk)
        q2 = q_ref[0, pl.ds(r0, qchunk), :]
        qblk = (qcol + r0) // L
        for half in range(2):
            hmask = (lane < E) if half == 0 else (lane >= E)
            qf = jnp.where(hmask, q2, 0.0)
            gate = lax.dot_general(kmean[...], qf, _NT, preferred_element_type=F32,
                                   precision=lax.Precision.HIGHEST)
            gate = jnp.where(blk_i < qblk, gate, NEG_BIG)
            picked = jnp.zeros((G, qchunk), jnp.bool_)
            for _ in range(topk):
                mx = jnp.max(gate, axis=0, keepdims=True)
                ix = jnp.min(jnp.where(gate == mx, blk_f, float(G)), axis=0, keepdims=True)
                hit = blk_f == ix
                picked = picked | (hit & (mx > 0.5 * NEG_BIG))
                gate = jnp.where(hit, -jnp.inf, gate)
            wanted = picked | (blk_i == qblk)
            sel_grp = 2 if half == 0 else 0
            place = (place_c == place_r + sel_grp * G).astype(BF16)
            flags = jnp.dot(wanted.astype(F32).T.astype(BF16), place, preferred_element_type=F32)
            ext = jnp.where((grp == sel_grp + 1) | ((grp == sel_grp) & (flags < 0.5)), NEG_BIG, 0.0)
            qx_ref[0, half, pl.ds(r0, qchunk), :] = jnp.where(hmask, qf * qscale, ext).astype(BF16)
        return carry

    lax.fori_loop(0, q_ref.shape[1] // qchunk, chunk_body, 0)


def _moba_body(qx, k_ref, v_ref, bias_ref, o_ref, kbh, vtb, sbuf, pbuf, acc,
               *, nb, ntile, unroll):
    L = C_BLOCK
    E = HEAD_DIM
    G = LANES // 4
    U = unroll
    assert U >= 2
    qb = pl.program_id(2)
    lane = lax.broadcasted_iota(jnp.int32, (L, LANES), 1)

    @pl.when(qb == 0)
    def _():
        ones_rows = jnp.ones((VT_ROWS - E, L), BF16)
        for n in range(nb):
            blk = k_ref[0, n * L:(n + 1) * L, :]
            kbh[n, 0] = jnp.where(lane < E, blk, (lane == 2 * G + n).astype(F32)).astype(BF16)
            kbh[n, 1] = jnp.where(lane >= E, blk, (lane == n).astype(F32)).astype(BF16)
            vt = v_ref[0, n * L:(n + 1) * L, :].T.astype(BF16)
            for half in range(2):
                vtb[n, half, 0:E, :] = vt[half * E:(half + 1) * E, :]
                vtb[n, half, E:, :] = ones_rows
        kbh[nb, 0] = (lane == 3 * G).astype(BF16)
        kbh[nb, 1] = (lane == G).astype(BF16)

    def scores_stage(dist, slot_):
        kb = jnp.where(dist <= qb, qb - dist, nb)
        tile = jnp.minimum(dist, ntile - 1)
        for half in range(2):
            s = lax.dot_general(kbh[kb, half], qx[0, half], _NT, preferred_element_type=F32)
            sbuf[slot_, half] = s + bias_ref[half, tile]

    def softmax_stage(slot_, m_old):
        out = []
        for half in range(2):
            s = sbuf[slot_, half]
            m_new = jnp.maximum(m_old[half], jnp.max(s, axis=0, keepdims=True))
            pbuf[slot_, half] = jnp.exp2(s - m_new).astype(BF16)
            out.append(m_new)
        return tuple(out)

    def pv_stage(dist, slot_, alpha):
        kb = jnp.clip(qb - dist, 0, nb - 1)
        for half in range(2):
            acc[half] = alpha[half] * acc[half] + jnp.dot(vtb[kb, half], pbuf[slot_, half],
                                                          preferred_element_type=F32)

    pbuf[...] = jnp.zeros_like(pbuf)
    acc[...] = jnp.zeros_like(acc)
    scores_stage(0, 0)

    def trip(j, carry):
        m_old, alpha = carry[0:2], carry[2:4]
        for u in range(U):
            dist = j * U + u
            pv_stage(dist - 1, (u - 1) % U, alpha)
            m_new = softmax_stage(u, m_old)
            scores_stage(dist + 1, (u + 1) % U)
            alpha = tuple(jnp.exp2(a - b) for a, b in zip(m_old, m_new))
            m_old = m_new
        return m_old + alpha

    row = jnp.full((1, L), NEG_BIG, F32)
    one = jnp.ones((1, L), F32)
    ntrips = (qb + U) // U
    carry = lax.fori_loop(0, ntrips, trip, (row, row, one, one))
    pv_stage(ntrips * U - 1, (U - 1) % U, carry[2:4])
    out_t = jnp.concatenate([acc[half, 0:E, :] / acc[half, E:E + 1, :] for half in range(2)], axis=0)
    o_ref[0] = out_t.T


def _moba(proj, bias_tiles, unroll=2):
    bsz, t, _ = proj.shape
    L = C_BLOCK
    assert t % L == 0
    nb = t // L
    assert nb <= LANES // 4, "one selection lane per key block and lane group"
    ntile = bias_tiles.shape[1]
    qcol, kcol, vcol = (C_COL0 // LANES, (C_COL0 + C_WIDTH) // LANES, (C_COL0 + 2 * C_WIDTH) // LANES)
    npair = C_HEADS // 2
    qchunk = min(2048, t)
    assert t % qchunk == 0
    qx = pl.pallas_call(
        functools.partial(_moba_select_body, nb=nb, topk=min(C_TOPK, nb), qchunk=qchunk),
        grid=(bsz, npair),
        in_specs=[
            pl.BlockSpec((1, t, LANES), lambda b, h: (b, 0, qcol + h)),
            pl.BlockSpec((1, t, LANES), lambda b, h: (b, 0, kcol + h)),
        ],
        out_specs=pl.BlockSpec((1, 2, t, LANES), lambda b, h: (b, h, 0, 0)),
        out_shape=jax.ShapeDtypeStruct((bsz, C_HEADS, t, LANES), BF16),
        scratch_shapes=[pltpu.VMEM((LANES // 4, LANES), F32)],
        compiler_params=pltpu.CompilerParams(
            dimension_semantics=("arbitrary", "arbitrary"), vmem_limit_bytes=VMEM_LIMIT),
        name="moba_select",
    )(proj, proj)
    return pl.pallas_call(
        functools.partial(_moba_body, nb=nb, ntile=ntile, unroll=unroll),
        grid=(bsz, npair, nb),
        in_specs=[
            pl.BlockSpec((1, 2, L, LANES), lambda b, h, i: (b, h, i, 0)),
            pl.BlockSpec((1, t, LANES), lambda b, h, i: (b, 0, kcol + h)),
            pl.BlockSpec((1, t, LANES), lambda b, h, i: (b, 0, vcol + h)),
            pl.BlockSpec((2, ntile, L, L), lambda b, h, i: (h, 0, 0, 0)),
        ],
        out_specs=pl.BlockSpec((1, L, LANES), lambda b, h, i: (b, i, h)),
        out_shape=jax.ShapeDtypeStruct((bsz, t, C_WIDTH), F32),
        scratch_shapes=[
            pltpu.VMEM((nb + 1, 2, L, LANES), BF16),
            pltpu.VMEM((nb, 2, VT_ROWS, L), BF16),
            pltpu.VMEM((unroll, 2, L, L), F32),
            pltpu.VMEM((unroll, 2, L, L), BF16),
            pltpu.VMEM((2, VT_ROWS, L), F32),
        ],
        compiler_params=pltpu.CompilerParams(
            dimension_semantics=("arbitrary", "arbitrary", "arbitrary"),
            vmem_limit_bytes=VMEM_LIMIT),
        name="moba",
    )(qx, proj, proj, bias_tiles)


def _post_body(x_ref, ya_ref, yb_ref, yc_ref, p_ref,
               wout_ref, nffn_ref, wup_ref, cw_ref, cb_ref, wdown_ref, nple_ref, wpg_ref, wpe_ref,
               nfin_ref, out_ref, carry_ref, *, final, fch):
    tm = x_ref.shape[1]

    @pl.when(pl.program_id(1) == 0)
    def _():
        carry_ref[...] = jnp.zeros_like(carry_ref)

    x = x_ref[0]
    x = x + jnp.dot(ya_ref[0].astype(BF16), wout_ref[0:A_WIDTH, :], preferred_element_type=F32)
    x = x + jnp.dot(yb_ref[0].astype(BF16), wout_ref[A_WIDTH:A_WIDTH + B_V_WIDTH, :],
                    preferred_element_type=F32)
    x = x + jnp.dot(yc_ref[0].astype(BF16), wout_ref[A_WIDTH + B_V_WIDTH:, :],
                    preferred_element_type=F32)

    h = _rms(x, nffn_ref[...]).astype(BF16)
    row = lax.broadcasted_iota(jnp.int32, (tm, 1), 0)

    def conv_cols(c0):
        cols = slice(c0, c0 + fch)
        u = jnp.dot(h, wup_ref[:, cols], preferred_element_type=F32)
        prev = carry_ref[:, cols]
        carry_ref[:, cols] = u[tm - 8:tm]
        u1 = jnp.where(row == 0, prev[7:8], pltpu.roll(u, 1, 0))
        u2 = jnp.where(row == 0, prev[6:7], jnp.where(row == 1, prev[7:8], pltpu.roll(u, 2, 0)))
        w = cw_ref[:, cols]
        return w[0:1] * u2 + w[1:2] * u1 + w[2:3] * u + cb_ref[:, cols]

    acc = jnp.zeros(x.shape, F32)
    for j in range(D_FF // fch):
        ug = conv_cols(j * fch)
        uv = conv_cols(D_FF + j * fch)
        act = (ug * _sigmoid(ug)) * uv
        acc = acc + jnp.dot(act.astype(BF16), wdown_ref[j * fch:(j + 1) * fch, :],
                            preferred_element_type=F32)
    x = x + acc

    hp = _rms(x, nple_ref[...]).astype(BF16)
    gate = _sigmoid(jnp.dot(hp, wpg_ref[...], preferred_element_type=F32))
    x = x + gate * jnp.dot(p_ref[0].astype(BF16), wpe_ref[...], preferred_element_type=F32)
    if final:
        x = _rms(x, nfin_ref[...])
    out_ref[0] = x


def _post(x, ya, yb, yc, p, wout, nffn, wup, cw, cb, wdown, nple, wpg, wpe, nfin, final,
          tm=512, fch=2816):
    bsz, t, d = x.shape
    assert t % tm == 0 and D_FF % fch == 0

    def tok(width):
        return pl.BlockSpec((1, tm, width), lambda b, i: (b, i, 0))

    def const(shape):
        return pl.BlockSpec(shape, lambda b, i: (0,) * len(shape), pipeline_mode=pl.Buffered(1))

    return pl.pallas_call(
        functools.partial(_post_body, final=final, fch=fch),
        grid=(bsz, t // tm),
        in_specs=[tok(d), tok(A_WIDTH), tok(B_V_WIDTH), tok(C_WIDTH), tok(PLE_DIM),
                  const(wout.shape), const(nffn.shape), const(wup.shape), const(cw.shape),
                  const(cb.shape), const(wdown.shape), const(nple.shape), const(wpg.shape),
                  const(wpe.shape), const(nfin.shape)],
        out_specs=tok(d),
        out_shape=jax.ShapeDtypeStruct((bsz, t, d), F32),
        scratch_shapes=[pltpu.VMEM((8, 2 * D_FF), F32)],
        compiler_params=pltpu.CompilerParams(
            dimension_semantics=("arbitrary", "arbitrary"), vmem_limit_bytes=VMEM_LIMIT),
        name="post",
    )(x, ya, yb, yc, p, wout, nffn, wup, cw, cb, wdown, nple, wpg, wpe, nfin)


def _dilated_bias(bias_a, window, dilation):
    L = A_BLOCK
    m = jnp.arange(3 * L - 1)
    u = bias_a[:, _rel_bucket((2 * L - 1 - m) * dilation)]
    tile = _toeplitz(u, L, 2 * L) * LOG2E
    qi = jnp.arange(L)[:, None]
    kj = jnp.arange(2 * L)[None, :]
    delta = qi + L - kj
    band = (delta >= 0) & (delta <= window // dilation)
    return jnp.stack([jnp.where(band, tile, NEG_BIG), jnp.where(band & (kj >= L), tile, NEG_BIG)])


def _moba_bias(bias_c, t):
    L = C_BLOCK
    nb = t // L
    dist = np.arange(0, t)
    bk = _rel_bucket_np(dist)
    nfar = nb
    for dblk in range(nb, 0, -1):
        if np.all(bk[max(dblk * L - (L - 1), 0):] == bk[-1]):
            nfar = dblk
    m = jnp.arange(2 * L - 1)
    tiles = []
    for dblk in range(nfar):
        u = bias_c[:, _rel_bucket(dblk * L + m - (L - 1))]
        tiles.append(_toeplitz(u, L, L))
    far = jnp.broadcast_to(bias_c[:, int(bk[-1])][:, None, None], (bias_c.shape[0], L, L))
    tiles.append(far)
    kj = jnp.arange(L)[:, None]
    qi = jnp.arange(L)[None, :]
    tiles[0] = jnp.where(qi >= kj, tiles[0] * LOG2E, NEG_BIG)
    tiles[1:] = [tl * LOG2E for tl in tiles[1:]]
    return jnp.stack(tiles, axis=1)


def kernel(x, p, rel_bias, norm_mix, w_in, w_out, lower_bounds, hgrn_gnorm, norm_ffn, w_up, conv_w,
           conv_b, w_down, norm_ple, w_pe, w_pg, norm_final):
    bsz, t, d = x.shape
    depth = w_in.shape[0]
    lb_sm = jax.nn.softmax(lower_bounds.astype(F32), axis=0)
    lb_all = jnp.cumsum(lb_sm, axis=0) - lb_sm[0]
    bias_a = rel_bias[:A_HEADS].astype(F32)
    bias_c = rel_bias[A_HEADS:].astype(F32)
    bias_a_tiles = jnp.stack([_dilated_bias(bias_a, win, dil) for win, dil in A_PATTERNS])
    bias_c_tiles = _moba_bias(bias_c, t)

    a_end = 3 * A_WIDTH
    b_end = a_end + B_GROUP_WIDTH
    for i in range(depth):
        w = w_in[i]
        w_perm = jnp.concatenate([w[:, a_end:b_end], w[:, :a_end], w[:, b_end:]], axis=1).astype(BF16)
        proj = _inproj(x.reshape(bsz * t, d), norm_mix[i][None], w_perm).reshape(bsz, t, IN_WIDTH)
        ya = _dilated(proj, bias_a_tiles)
        yb = _hgrn(proj, lb_all[i][None], hgrn_gnorm[i].astype(F32)[None])
        yc = _moba(proj, bias_c_tiles)
        x = _post(x, ya, yb, yc, p[i], w_out[i].astype(BF16), norm_ffn[i][None],
                  w_up[i].astype(BF16), conv_w[i], conv_b[i][None], w_down[i].astype(BF16),
                  norm_ple[i][None], w_pg[i].astype(BF16), w_pe[i].astype(BF16), norm_final[None],
                  final=(i == depth - 1))
    return x
```
